```python
import jax, jax.numpy as jnp
from jax import lax
import numpy as np

D_MODEL = 1024
BATCH = 4
SEQ = 4096
DEPTH = 2

GRID_W = 64
CTX_LEN = 256
Q_BLOCK = 128
ROPE_BASE = 10000.0
EPS = 1e-6

MLA_HEADS = 8
MLA_NOPE = 64
MLA_ROPE = 32
MLA_QK = MLA_NOPE + MLA_ROPE
MLA_V = 64
MLA_Q_RANK = 384
MLA_KV_RANK = 256
GQA_HEADS = 8
GQA_KV_HEADS = 2
GQA_GROUP = GQA_HEADS // GQA_KV_HEADS
GQA_DIM = 64
FOUR_GROUPS = 4
FOUR_GROUP_DIM = 128
FOUR_WIDTH = FOUR_GROUPS * FOUR_GROUP_DIM
D_FF = 2816
CONV_W = 3
N_BRANCH = 3

KV_COLS = MLA_KV_RANK + MLA_ROPE + 2 * GQA_KV_HEADS * GQA_DIM
Q_COLS = MLA_Q_RANK + GQA_HEADS * GQA_DIM
FOUR_START = KV_COLS + Q_COLS
GATE_START = FOUR_START + FOUR_WIDTH
IN_COLS = GATE_START + N_BRANCH * D_MODEL

kernel_name = "hybrid_mla_gqa_fnet_convffn_prefix_block"


def rms_norm(x, g):
    xf = x.astype(jnp.float32)
    y = xf * lax.rsqrt(jnp.mean(xf * xf, axis=-1, keepdims=True) + EPS)
    return (y * g.astype(jnp.float32)).astype(x.dtype)


def modulate(x, g, shift, scale):
    return rms_norm(x, g) * (1 + scale) + shift


def axial_rope_tables(rows, rot_dim, dtype):
    row = jnp.repeat(jnp.arange(rows, dtype=jnp.float32), GRID_W)
    col = jnp.tile(jnp.arange(GRID_W, dtype=jnp.float32), rows)
    n_f = rot_dim // 4
    inv = ROPE_BASE ** (-jnp.arange(n_f, dtype=jnp.float32) / n_f)
    ang = jnp.concatenate([row[:, None] * inv, col[:, None] * inv], axis=-1)
    return (jnp.cos(ang)[:, None, :].astype(dtype), jnp.sin(ang)[:, None, :].astype(dtype))


def apply_rope(x, cs):
    cos, sin = cs
    x1, x2 = jnp.split(x, 2, axis=-1)
    return jnp.concatenate([x1 * cos - x2 * sin, x2 * cos + x1 * sin], axis=-1)


def attend(q, k, v, scale):
    s = jnp.einsum("bqkgd,blkd->bkgql", q, k).astype(jnp.float32) * scale
    p = jax.nn.softmax(s, axis=-1).astype(v.dtype)
    return jnp.einsum("bkgql,blkd->bqkgd", p, v)


def blocked_attend(q, k, v, scale):
    b, lq, hk, g, dq = q.shape
    nb = lq // Q_BLOCK
    qb = jnp.moveaxis(q.reshape(b, nb, Q_BLOCK, hk, g, dq), 1, 0)
    ob = lax.map(lambda qi: attend(qi, k, v, scale), qb)
    return jnp.moveaxis(ob, 0, 1).reshape(b, lq, hk, g, v.shape[-1])


def mixer_kv(kvc, lp, rope_mla, rope_gqa):
    b, l, _ = kvc.shape
    c_kv, k_rope, k_g, v_g = jnp.split(
        kvc, [MLA_KV_RANK, MLA_KV_RANK + MLA_ROPE, MLA_KV_RANK + MLA_ROPE + GQA_KV_HEADS * GQA_DIM], axis=-1)
    kv = (rms_norm(c_kv, lp["g_ckv"]) @ lp["w_ukv"]).reshape(b, l, MLA_HEADS, MLA_NOPE + MLA_V)
    k_nope, v_mla = jnp.split(kv, [MLA_NOPE], axis=-1)
    k_rope = jnp.broadcast_to(k_rope[:, :, None, :], (b, l, MLA_HEADS, MLA_ROPE))
    k_mla = rms_norm(jnp.concatenate([k_nope, k_rope], axis=-1), lp["g_kn_mla"])
    k_gqa = rms_norm(k_g.reshape(b, l, GQA_KV_HEADS, GQA_DIM), lp["g_kn_gqa"])
    v_gqa = v_g.reshape(b, l, GQA_KV_HEADS, GQA_DIM)
    if rope_mla is not None:
        k_mla = jnp.concatenate([k_mla[..., :MLA_NOPE], apply_rope(k_mla[..., MLA_NOPE:], rope_mla)], axis=-1)
        k_gqa = apply_rope(k_gqa, rope_gqa)
    return k_mla, v_mla, k_gqa, v_gqa


def mixer_q(qc, lp, rope_mla, rope_gqa):
    b, l, _ = qc.shape
    c_q, q_g = jnp.split(qc, [MLA_Q_RANK], axis=-1)
    q_mla = (rms_norm(c_q, lp["g_cq"]) @ lp["w_uq"]).reshape(b, l, MLA_HEADS, MLA_QK)
    q_mla = rms_norm(q_mla, lp["g_qn_mla"])
    q_gqa = rms_norm(q_g.reshape(b, l, GQA_HEADS, GQA_DIM), lp["g_qn_gqa"])
    if rope_mla is not None:
        q_mla = jnp.concatenate([q_mla[..., :MLA_NOPE], apply_rope(q_mla[..., MLA_NOPE:], rope_mla)], axis=-1)
        q_gqa = apply_rope(q_gqa, rope_gqa)
    return q_mla[:, :, :, None, :], q_gqa.reshape(b, l, GQA_KV_HEADS, GQA_GROUP, GQA_DIM)


def fourier_mix(f):
    b, l, _ = f.shape
    z = jnp.fft.fftn(f.astype(jnp.float32).reshape(b, l, FOUR_GROUPS, FOUR_GROUP_DIM), axes=(1, 3), norm="ortho")
    return jnp.real(z).astype(f.dtype).reshape(b, l, FOUR_WIDTH)


def merge_branches(o_mla, o_gqa, o_four, gate_cols, lp):
    b, l = o_four.shape[:2]
    g_mla, g_gqa, g_four = jnp.split(jax.nn.sigmoid(gate_cols), N_BRANCH, axis=-1)
    y = (g_mla * (o_mla.reshape(b, l, MLA_HEADS * MLA_V) @ lp["w_br_mla"])
         + g_gqa * (o_gqa.reshape(b, l, GQA_HEADS * GQA_DIM) @ lp["w_br_gqa"])
         + g_four * (o_four @ lp["w_four"]))
    return y @ lp["w_o"]


def conv_ffn(h, lp):
    u = h @ lp["w_up"]
    ch = u.shape[-1]
    u = lax.conv_general_dilated(
        u, lp["conv_w"][:, None, :].astype(u.dtype), window_strides=(1,),
        padding=((CONV_W // 2, CONV_W // 2),), dimension_numbers=("NWC", "WIO", "NWC"),
        feature_group_count=ch) + lp["conv_b"]
    a, v = jnp.split(u, 2, axis=-1)
    return (jax.nn.silu(a) * v) @ lp["w_down"]


def layer(x, xc, silu_c, silu_cc, lp, rope_mla, rope_gqa, update_ctx):
    d = D_MODEL
    mod = (silu_c @ lp["w_mod"] + lp["b_mod"])[:, None, :]
    sh1, sc1, gt1, sh2, sc2, gt2 = jnp.split(mod, 6, axis=-1)
    n_mod_c = 6 * d if update_ctx else 2 * d
    modc = (silu_cc @ lp["w_mod"][:, :n_mod_c] + lp["b_mod"][:n_mod_c])[None, None, :]

    h = modulate(x, lp["g_norm1"], sh1, sc1)
    hc = modulate(xc, lp["g_norm1"], modc[..., :d], modc[..., d:2 * d])
    proj = h @ lp["w_in"]
    n_in_c = IN_COLS if update_ctx else KV_COLS
    projc = hc @ lp["w_in"][:, :n_in_c]

    k_mla_l, v_mla_l, k_gqa_l, v_gqa_l = mixer_kv(proj[..., :KV_COLS], lp, rope_mla, rope_gqa)
    k_mla_c, v_mla_c, k_gqa_c, v_gqa_c = mixer_kv(projc[..., :KV_COLS], lp, None, None)
    q_mla, q_gqa = mixer_q(proj[..., KV_COLS:FOUR_START], lp, rope_mla, rope_gqa)

    o_mla = blocked_attend(q_mla, jnp.concatenate([k_mla_c, k_mla_l], axis=1),
                           jnp.concatenate([v_mla_c, v_mla_l], axis=1), MLA_QK ** -0.5)
    o_gqa = blocked_attend(q_gqa, jnp.concatenate([k_gqa_c, k_gqa_l], axis=1),
                           jnp.concatenate([v_gqa_c, v_gqa_l], axis=1), GQA_DIM ** -0.5)
    o_four = fourier_mix(proj[..., FOUR_START:GATE_START])
    x = x + gt1 * merge_branches(o_mla, o_gqa, o_four, proj[..., GATE_START:], lp)
    x = x + gt2 * conv_ffn(modulate(x, lp["g_norm2"], sh2, sc2), lp)

    if update_ctx:
        c_gt1, c_sh2, c_sc2, c_gt2 = jnp.split(modc[..., 2 * d:], 4, axis=-1)
        qc_mla, qc_gqa = mixer_q(projc[..., KV_COLS:FOUR_START], lp, None, None)
        oc_mla = attend(qc_mla, k_mla_c, v_mla_c, MLA_QK ** -0.5)
        oc_gqa = attend(qc_gqa, k_gqa_c, v_gqa_c, GQA_DIM ** -0.5)
        oc_four = fourier_mix(projc[..., FOUR_START:GATE_START])
        xc = xc + c_gt1 * merge_branches(oc_mla, oc_gqa, oc_four, projc[..., GATE_START:], lp)
        xc = xc + c_gt2 * conv_ffn(modulate(xc, lp["g_norm2"], c_sh2, c_sc2), lp)
    return x, xc


def setup_inputs(seed: int = 0) -> dict:
    key = jax.random.key(seed)
    ks = jax.random.split(key, 32)
    f32 = jnp.float32

    def w(k, shape, fan_in, gain=1.0):
        return (jax.random.normal(k, shape, f32) * (gain * fan_in ** -0.5)).astype(f32)

    def gain(k, shape):
        return 1.0 + 0.02 * jax.random.normal(k, shape, f32)

    d = D_MODEL
    return {
        "x": jax.random.normal(ks[0], (BATCH, SEQ, d), f32),
        "c": jax.random.normal(ks[1], (BATCH, d), f32),
        "ctx": jax.random.normal(ks[2], (BATCH, CTX_LEN, d), f32),
        "c_ctx": jax.random.normal(ks[3], (d,), f32),
        "w_mod": w(ks[4], (DEPTH, d, 6 * d), d, 0.5),
        "b_mod": 0.01 * jax.random.normal(ks[5], (DEPTH, 6 * d), f32),
        "g_norm1": gain(ks[6], (DEPTH, d)),
        "g_norm2": gain(ks[7], (DEPTH, d)),
        "w_in": w(ks[8], (DEPTH, d, IN_COLS), d),
        "g_cq": gain(ks[9], (DEPTH, MLA_Q_RANK)),
        "g_ckv": gain(ks[10], (DEPTH, MLA_KV_RANK)),
        "w_uq": w(ks[11], (DEPTH, MLA_Q_RANK, MLA_HEADS * MLA_QK), MLA_Q_RANK),
        "w_ukv": w(ks[12], (DEPTH, MLA_KV_RANK, MLA_HEADS * (MLA_NOPE + MLA_V)), MLA_KV_RANK),
        "g_qn_mla": gain(ks[13], (DEPTH, MLA_QK)),
        "g_kn_mla": gain(ks[14], (DEPTH, MLA_QK)),
        "g_qn_gqa": gain(ks[15], (DEPTH, GQA_DIM)),
        "g_kn_gqa": gain(ks[16], (DEPTH, GQA_DIM)),
        "w_br_mla": w(ks[17], (DEPTH, MLA_HEADS * MLA_V, d), MLA_HEADS * MLA_V),
        "w_br_gqa": w(ks[18], (DEPTH, GQA_HEADS * GQA_DIM, d), GQA_HEADS * GQA_DIM),
        "w_four": w(ks[19], (DEPTH, FOUR_WIDTH, d), FOUR_WIDTH),
        "w_o": w(ks[20], (DEPTH, d, d), d),
        "w_up": w(ks[21], (DEPTH, d, 2 * D_FF), d),
        "conv_w": w(ks[22], (DEPTH, CONV_W, 2 * D_FF), CONV_W),
        "conv_b": 0.01 * jax.random.normal(ks[23], (DEPTH, 2 * D_FF), f32),
        "w_down": w(ks[24], (DEPTH, D_FF, d), D_FF),
    }


def reference(x, c, ctx, c_ctx, w_mod, b_mod, g_norm1, g_norm2, w_in, g_cq, g_ckv, w_uq, w_ukv,
              g_qn_mla, g_kn_mla, g_qn_gqa, g_kn_gqa, w_br_mla, w_br_gqa, w_four, w_o,
              w_up, conv_w, conv_b, w_down):
    n_tok = x.shape[1]
    rows = n_tok // GRID_W
    rope_mla = axial_rope_tables(rows, MLA_ROPE, x.dtype)
    rope_gqa = axial_rope_tables(rows, GQA_DIM, x.dtype)
    silu_c = jax.nn.silu(c)
    silu_cc = jax.nn.silu(c_ctx)
    xc = ctx
    for l in range(DEPTH):
        lp = {
            "w_mod": w_mod[l], "b_mod": b_mod[l], "g_norm1": g_norm1[l], "g_norm2": g_norm2[l],
            "w_in": w_in[l], "g_cq": g_cq[l], "g_ckv": g_ckv[l], "w_uq": w_uq[l], "w_ukv": w_ukv[l],
            "g_qn_mla": g_qn_mla[l], "g_kn_mla": g_kn_mla[l], "g_qn_gqa": g_qn_gqa[l], "g_kn_gqa": g_kn_gqa[l],
            "w_br_mla": w_br_mla[l], "w_br_gqa": w_br_gqa[l], "w_four": w_four[l], "w_o": w_o[l],
            "w_up": w_up[l], "conv_w": conv_w[l], "conv_b": conv_b[l], "w_down": w_down[l],
        }
        x, xc = layer(x, xc, silu_c, silu_cc, lp, rope_mla, rope_gqa, l < DEPTH - 1)
    return x
```

```python
import functools

import numpy as np
import jax
import jax.numpy as jnp
from jax import lax
from jax.experimental import pallas as pl
from jax.experimental.pallas import tpu as pltpu

F32 = jnp.float32
BF16 = jnp.bfloat16

LANES = 128
SUBLANES = 8
VMEM_LIMIT = 56 * 1024 * 1024

GRID_W = 64
ROPE_BASE = 10000.0
EPS = 1e-6
MLA_HEADS = 8
MLA_NOPE = 64
MLA_ROPE = 32
MLA_QK = MLA_NOPE + MLA_ROPE
MLA_V = 64
MLA_Q_RANK = 384
MLA_KV_RANK = 256
GQA_HEADS = 8
GQA_KV_HEADS = 2
GQA_DIM = 64
FOUR_GROUPS = 4
FOUR_GROUP_DIM = 128
FOUR_WIDTH = FOUR_GROUPS * FOUR_GROUP_DIM
N_BRANCH = 3

C_CKV = 0
C_KROPE = C_CKV + MLA_KV_RANK
C_KG = C_KROPE + LANES
C_VG = C_KG + GQA_KV_HEADS * LANES
C_KV_END = C_VG + 2 * GQA_KV_HEADS * LANES
C_CQ = C_KV_END
C_QG = C_CQ + MLA_Q_RANK
C_FOUR = C_QG + GQA_HEADS * LANES
C_END = C_FOUR + FOUR_WIDTH


def _cparams(n_grid, vmem=VMEM_LIMIT):
    return pltpu.CompilerParams(dimension_semantics=("arbitrary",) * n_grid, vmem_limit_bytes=vmem)


def _const_spec(shape):
    nd = len(shape)
    return pl.BlockSpec(shape, lambda *_: (0,) * nd, pipeline_mode=pl.Buffered(1))


def _rms(x, g, dim):
    ss = jnp.sum(x * x, axis=-1, keepdims=True)
    return x * lax.rsqrt(ss * (1.0 / dim) + EPS) * g


def _modulate(x, g, shift, scale):
    return _rms(x, g, x.shape[-1]) * (1.0 + scale) + shift


def _rope_slot(x, tab_ref, half):
    up = pltpu.roll(x, LANES - half, axis=1)
    dn = pltpu.roll(x, half, axis=1)
    return x * tab_ref[0] + up * tab_ref[1] + dn * tab_ref[2]


def _split(x):
    hi = x.astype(BF16)
    lo = (x - hi.astype(F32)).astype(BF16)
    return hi, lo


def _dot(a, b):
    return jnp.dot(a, b, preferred_element_type=F32)


def _dot3_lhs_const(c_ref, x):
    xh, xl = _split(x)
    return _dot(c_ref[0], xh) + _dot(c_ref[1], xh) + _dot(c_ref[0], xl)


def _dot3_rhs_const(x, c_ref):
    xh, xl = _split(x)
    return _dot(xh, c_ref[0]) + _dot(xh, c_ref[1]) + _dot(xl, c_ref[0])


def _mod_kernel(c_ref, w_ref, b_ref, o_ref):
    c = c_ref[...]
    s = (c * (1.0 / (1.0 + jnp.exp(-c)))).astype(BF16)
    o_ref[...] = _dot(s, w_ref[...].astype(BF16)) + b_ref[...]


def _mod_call(c_all, w_mod, b_mod):
    depth, d, n = w_mod.shape
    tn = n // 4
    return pl.pallas_call(
        _mod_kernel,
        grid=(depth, n // tn),
        in_specs=[pl.BlockSpec(c_all.shape, lambda l, j: (0, 0)),
                  pl.BlockSpec((None, d, tn), lambda l, j: (l, 0, j)),
                  pl.BlockSpec((None, 1, tn), lambda l, j: (l, 0, j))],
        out_specs=pl.BlockSpec((None, c_all.shape[0], tn), lambda l, j: (l, 0, j)),
        out_shape=jax.ShapeDtypeStruct((depth, c_all.shape[0], n), F32),
        compiler_params=_cparams(2),
        name="modulation",
    )(c_all, w_mod, b_mod.reshape(depth, 1, n))


def _inproj_kernel(mod_ref, x_ref, g1_ref, w1_ref, gckv_ref, gcq_ref, wukv_ref, wuq_ref, gslot_ref,
                   rm_ref, rg_ref, *out_refs, mod_row, rope, kv_only):
    d = x_ref.shape[-1]
    b = pl.program_id(0) if mod_row is None else mod_row
    shift = mod_ref[pl.ds(b, 1), 0:d]
    scale = mod_ref[pl.ds(b, 1), d:2 * d]
    h = _modulate(x_ref[...], g1_ref[...], shift, scale).astype(BF16)
    p = _dot(h, w1_ref[...])
    km_ref, vm_ref, kg_ref, vg_ref = out_refs[:4]

    ckv = _rms(p[:, C_CKV:C_CKV + MLA_KV_RANK], gckv_ref[...], MLA_KV_RANK).astype(BF16)
    kv = _dot(ckv, wukv_ref[...])
    krope = p[:, C_KROPE:C_KROPE + LANES]
    for hd in range(MLA_HEADS):
        kh = _rms(kv[:, hd * LANES:(hd + 1) * LANES] + krope, gslot_ref[0:1, :], MLA_QK)
        if rope:
            kh = _rope_slot(kh, rm_ref, MLA_ROPE // 2)
        km_ref[:, hd * LANES:(hd + 1) * LANES] = kh.astype(km_ref.dtype)
    vm_ref[...] = kv[:, MLA_HEADS * LANES:].astype(vm_ref.dtype)

    for hd in range(GQA_KV_HEADS):
        kh = _rms(p[:, C_KG + hd * LANES:C_KG + (hd + 1) * LANES], gslot_ref[2:3, :], GQA_DIM)
        if rope:
            kh = _rope_slot(kh, rg_ref, GQA_DIM // 2)
        kg_ref[:, hd * LANES:(hd + 1) * LANES] = kh.astype(kg_ref.dtype)
    vg_ref[...] = p[:, C_VG:C_KV_END].astype(vg_ref.dtype)
    if kv_only:
        return

    qm_ref, qg_ref, four_ref = out_refs[4:]
    cq = _rms(p[:, C_CQ:C_CQ + MLA_Q_RANK], gcq_ref[...], MLA_Q_RANK).astype(BF16)
    qm = _dot(cq, wuq_ref[...])
    for hd in range(MLA_HEADS):
        qh = _rms(qm[:, hd * LANES:(hd + 1) * LANES], gslot_ref[1:2, :], MLA_QK)
        if rope:
            qh = _rope_slot(qh, rm_ref, MLA_ROPE // 2)
        qm_ref[:, hd * LANES:(hd + 1) * LANES] = qh.astype(qm_ref.dtype)
    for hd in range(GQA_HEADS):
        qh = _rms(p[:, C_QG + hd * LANES:C_QG + (hd + 1) * LANES], gslot_ref[3:4, :], GQA_DIM)
        if rope:
            qh = _rope_slot(qh, rg_ref, GQA_DIM // 2)
        qg_ref[:, hd * LANES:(hd + 1) * LANES] = qh.astype(qg_ref.dtype)
    four_ref[...] = p[:, C_FOUR:C_END]


def _inproj_call(x, mod, wp, tabs, *, mod_row, rope, kv_only, tm):
    bsz, seq, d = x.shape
    ncol = C_KV_END if kv_only else C_END
    kernel = functools.partial(_inproj_kernel, mod_row=mod_row, rope=rope, kv_only=kv_only)
    row = lambda width, dt: (pl.BlockSpec((None, tm, width), lambda b, i: (b, i, 0)),
                             jax.ShapeDtypeStruct((bsz, seq, width), dt))
    outs = [row(MLA_HEADS * LANES, BF16), row(MLA_HEADS * LANES, BF16),
            row(GQA_KV_HEADS * LANES, BF16), row(2 * GQA_KV_HEADS * LANES, BF16)]
    if not kv_only:
        outs += [row(MLA_HEADS * LANES, BF16), row(GQA_HEADS * LANES, BF16), row(FOUR_WIDTH, F32)]
    tab_spec = pl.BlockSpec((3, tm, LANES), lambda b, i: (0, i, 0))
    return pl.pallas_call(
        kernel,
        grid=(bsz, seq // tm),
        in_specs=[_const_spec(mod.shape),
                  pl.BlockSpec((None, tm, d), lambda b, i: (b, i, 0)),
                  _const_spec((1, d)),
                  pl.BlockSpec((d, ncol), lambda b, i: (0, 0), pipeline_mode=pl.Buffered(1)),
                  _const_spec((1, MLA_KV_RANK)), _const_spec((1, MLA_Q_RANK)),
                  _const_spec(wp["w_ukv"].shape), _const_spec(wp["w_uq"].shape),
                  _const_spec(wp["gslot"].shape), tab_spec, tab_spec],
        out_specs=[o[0] for o in outs],
        out_shape=[o[1] for o in outs],
        compiler_params=_cparams(2),
        name="inproj_kv" if kv_only else "inproj",
    )(mod, x, wp["g_norm1"], wp["w1"], wp["g_ckv"], wp["g_cq"], wp["w_ukv"], wp["w_uq"], wp["gslot"],
      tabs[0], tabs[1])


def _attn_kernel(q_ref, *refs, n_src, n_heads, shared_kv):
    kv_refs, o_ref = refs[:2 * n_src], refs[2 * n_src]
    nt = (((1,), (1,)), ((), ()))
    for pair in range(n_heads // 2):
        acc = None
        for hh in range(2):
            i = pair * 2 + hh
            q = q_ref[:, i * LANES:(i + 1) * LANES]
            kcol = 0 if shared_kv else i * LANES
            vcol = hh * LANES if shared_kv else i * LANES
            ss = [lax.dot_general(q, kv_refs[2 * s][:, kcol:kcol + LANES], nt, preferred_element_type=F32)
                  for s in range(n_src)]
            m = functools.reduce(jnp.maximum, [jnp.max(s, axis=-1, keepdims=True) for s in ss])
            ps = [jnp.exp(s - m) for s in ss]
            l = functools.reduce(jnp.add, [jnp.sum(p, axis=-1, keepdims=True) for p in ps])
            o = functools.reduce(jnp.add, [_dot(ps[s].astype(BF16), kv_refs[2 * s + 1][:, vcol:vcol + LANES])
                                           for s in range(n_src)])
            o = o / l
            acc = o if acc is None else acc + o
        o_ref[:, pair * LANES:(pair + 1) * LANES] = acc.astype(o_ref.dtype)


def _attn_call(q, srcs, *, shared_kv, tq, name):
    bsz, lq, qw = q.shape
    n_heads = 4 if shared_kv else 2
    n_groups = qw // (n_heads * LANES)
    kw = LANES if shared_kv else 2 * LANES
    vw = 2 * LANES
    in_specs = [pl.BlockSpec((None, tq, n_heads * LANES), lambda b, g, i: (b, i, g))]
    args = [q]
    for k, v in srcs:
        lk = k.shape[1]
        in_specs += [pl.BlockSpec((None, lk, kw), lambda b, g, i: (b, 0, g)),
                     pl.BlockSpec((None, lk, vw), lambda b, g, i: (b, 0, g))]
        args += [k, v]
    ow = n_heads // 2 * LANES
    return pl.pallas_call(
        functools.partial(_attn_kernel, n_src=len(srcs), n_heads=n_heads, shared_kv=shared_kv),
        grid=(bsz, n_groups, lq // tq),
        in_specs=in_specs,
        out_specs=pl.BlockSpec((None, tq, ow), lambda b, g, i: (b, i, g)),
        out_shape=jax.ShapeDtypeStruct((bsz, lq, n_groups * ow), BF16),
        compiler_params=_cparams(3),
        name=name,
    )(*args)


def _dft_rows_kernel(fa_ref, x_ref, y_ref):
    y_ref[...] = _dot3_lhs_const(fa_ref, x_ref[...])


def _dft_cols_kernel(fb_ref, cs_ref, tc_ref, ts_ref, y_ref, o_ref):
    nr = y_ref.shape[1]
    reps = FOUR_WIDTH // LANES
    for r in range(nr):
        yr, yi = y_ref[0, r], y_ref[1, r]
        tc = jnp.concatenate([tc_ref[r]] * reps, axis=1)
        ts = jnp.concatenate([ts_ref[r]] * reps, axis=1)
        z = jnp.concatenate([yr * tc - yi * ts, yr * ts + yi * tc], axis=0)
        o2 = _dot3_lhs_const(fb_ref, z)
        half = o2.shape[0] // 2
        ocat = jnp.concatenate([o2[:half], o2[half:]], axis=1)
        o_ref[:, r * FOUR_WIDTH:(r + 1) * FOUR_WIDTH] = _dot3_rhs_const(ocat, cs_ref)


def _dft_small_kernel(f_ref, cs_ref, x_ref, o_ref):
    x2 = _dot3_lhs_const(f_ref, x_ref[...])
    half = x2.shape[0] // 2
    ocat = jnp.concatenate([x2[:half], x2[half:]], axis=1)
    o_ref[...] = _dot3_rhs_const(ocat, cs_ref)


def _np_split(a):
    a = np.asarray(a, np.float32)
    hi = jnp.asarray(a, F32).astype(BF16)
    lo = (jnp.asarray(a, F32) - hi.astype(F32)).astype(BF16)
    return jnp.stack([hi, lo])


def _dft_cos_sin(n):
    idx = np.arange(n)
    ang = 2.0 * np.pi * ((idx[:, None] * idx[None, :]) % n) / n
    return np.cos(ang), np.sin(ang)


def _channel_dft_const(seq):
    c, s = _dft_cos_sin(FOUR_GROUP_DIM)
    eye = np.eye(FOUR_GROUPS)
    norm = 1.0 / np.sqrt(seq * FOUR_GROUP_DIM)
    return _np_split(np.concatenate([np.kron(eye, c), np.kron(eye, s)], axis=0) * norm)


def _fourier_call(f):
    bsz, seq, w = f.shape
    cs = _channel_dft_const(seq)
    if seq < GRID_W * SUBLANES:
        c, s = _dft_cos_sin(seq)
        fmat = _np_split(np.concatenate([c, -s], axis=0))
        return pl.pallas_call(
            _dft_small_kernel,
            grid=(bsz,),
            in_specs=[_const_spec(fmat.shape), _const_spec(cs.shape),
                      pl.BlockSpec((None, seq, w), lambda b: (b, 0, 0))],
            out_specs=pl.BlockSpec((None, seq, w), lambda b: (b, 0, 0)),
            out_shape=jax.ShapeDtypeStruct((bsz, seq, w), F32),
            compiler_params=_cparams(1),
            name="dft_small",
        )(fmat, cs, f)

    rows = seq // GRID_W
    ca, sa = _dft_cos_sin(rows)
    fa = _np_split(np.concatenate([ca, -sa], axis=0))
    cb, sb = _dft_cos_sin(GRID_W)
    fb = _np_split(np.block([[cb, sb], [-sb, cb]]))
    ang = 2.0 * np.pi * (np.arange(rows)[:, None] * np.arange(GRID_W)[None, :]) / seq
    tc = jnp.asarray(np.repeat(np.cos(ang)[:, :, None], LANES, axis=2), F32)
    ts = jnp.asarray(np.repeat(-np.sin(ang)[:, :, None], LANES, axis=2), F32)

    nb = SUBLANES
    y = pl.pallas_call(
        _dft_rows_kernel,
        grid=(bsz, GRID_W // nb),
        in_specs=[_const_spec(fa.shape),
                  pl.BlockSpec((None, rows, nb * w), lambda b, j: (b, 0, j))],
        out_specs=pl.BlockSpec((None, 2 * rows, nb * w), lambda b, j: (b, 0, j)),
        out_shape=jax.ShapeDtypeStruct((bsz, 2 * rows, GRID_W * w), F32),
        compiler_params=_cparams(2),
        name="dft_rows",
    )(fa, f.reshape(bsz, rows, GRID_W * w))

    nr = SUBLANES
    o = pl.pallas_call(
        _dft_cols_kernel,
        grid=(bsz, rows // nr),
        in_specs=[_const_spec(fb.shape), _const_spec(cs.shape),
                  pl.BlockSpec((nr, GRID_W, LANES), lambda b, j: (j, 0, 0)),
                  pl.BlockSpec((nr, GRID_W, LANES), lambda b, j: (j, 0, 0)),
                  pl.BlockSpec((None, 2, nr, GRID_W, w), lambda b, j: (b, 0, j, 0, 0))],
        out_specs=pl.BlockSpec((None, GRID_W, nr * w), lambda b, j: (b, 0, j)),
        out_shape=jax.ShapeDtypeStruct((bsz, GRID_W, rows * w), F32),
        compiler_params=_cparams(2),
        name="dft_cols",
    )(fb, cs, tc, ts, y.reshape(bsz, 2, rows, GRID_W, w))
    return o.reshape(bsz, seq, w)


def _merge_kernel(mod_ref, x_ref, g1_ref, om_ref, og_ref, of_ref, wg_ref, wbm_ref, wbg_ref, wbf_ref, wo_ref,
                  o_ref, *, mod_row):
    d = x_ref.shape[-1]
    b = pl.program_id(0) if mod_row is None else mod_row
    shift = mod_ref[pl.ds(b, 1), 0:d]
    scale = mod_ref[pl.ds(b, 1), d:2 * d]
    gate1 = mod_ref[pl.ds(b, 1), 2 * d:3 * d]
    x = x_ref[...]
    h = _modulate(x, g1_ref[...], shift, scale).astype(BF16)
    y = None
    branches = ((om_ref[...], wbm_ref), (og_ref[...], wbg_ref), (of_ref[...].astype(BF16), wbf_ref))
    for n, (o, w_ref) in enumerate(branches):
        gate = 1.0 / (1.0 + jnp.exp(-_dot(h, wg_ref[:, n * d:(n + 1) * d])))
        t = gate * _dot(o, w_ref[...])
        y = t if y is None else y + t
    o_ref[...] = x + gate1 * _dot(y.astype(BF16), wo_ref[...])


def _merge_call(x, mod, o_mla, o_gqa, o_four, wp, *, mod_row, tm):
    bsz, seq, d = x.shape
    row = lambda width: pl.BlockSpec((None, tm, width), lambda b, i: (b, i, 0))
    return pl.pallas_call(
        functools.partial(_merge_kernel, mod_row=mod_row),
        grid=(bsz, seq // tm),
        in_specs=[_const_spec(mod.shape), row(d), _const_spec((1, d)),
                  row(o_mla.shape[-1]), row(o_gqa.shape[-1]), row(o_four.shape[-1]),
                  _const_spec(wp["w_gate"].shape), _const_spec(wp["w_br_mla"].shape),
                  _const_spec(wp["w_br_gqa"].shape), _const_spec(wp["w_four"].shape),
                  _const_spec(wp["w_o"].shape)],
        out_specs=row(d),
        out_shape=jax.ShapeDtypeStruct(x.shape, F32),
        compiler_params=_cparams(2),
        name="merge",
    )(mod, x, wp["g_norm1"], o_mla, o_gqa, o_four, wp["w_gate"], wp["w_br_mla"], wp["w_br_gqa"],
      wp["w_four"], wp["w_o"])


def _ffn_kernel(mod_ref, xp_ref, x_ref, xn_ref, g2_ref, wup_ref, cw_ref, cb_ref, wdn_ref, o_ref, *, mod_row):
    d = x_ref.shape[-1]
    tm = x_ref.shape[0]
    halo = xp_ref.shape[0]
    b = pl.program_id(0) if mod_row is None else mod_row
    i = pl.program_id(1)
    shift = mod_ref[pl.ds(b, 1), 3 * d:4 * d]
    scale = mod_ref[pl.ds(b, 1), 4 * d:5 * d]
    gate2 = mod_ref[pl.ds(b, 1), 5 * d:6 * d]
    x = x_ref[...]
    xe = jnp.concatenate([xp_ref[...], x, xn_ref[...]], axis=0)
    he = _modulate(xe, g2_ref[...], shift, scale)
    rid = lax.broadcasted_iota(jnp.int32, (tm + 2 * halo, 1), 0)
    outside = ((rid < halo) & (i == 0)) | ((rid >= tm + halo) & (i == pl.num_programs(1) - 1))
    he = jnp.where(outside, 0.0, he).astype(BF16)
    u = _dot(he, wup_ref[...])
    n = tm + 2 * halo
    up = pltpu.roll(u, 1, axis=0)[halo:halo + tm]
    dn = pltpu.roll(u, n - 1, axis=0)[halo:halo + tm]
    u = cw_ref[0:1, :] * up + cw_ref[1:2, :] * u[halo:halo + tm] + cw_ref[2:3, :] * dn + cb_ref[...]
    ff = u.shape[-1] // 2
    a, v = u[:, :ff], u[:, ff:]
    act = (a * (1.0 / (1.0 + jnp.exp(-a))) * v).astype(BF16)
    o_ref[...] = x + gate2 * _dot(act, wdn_ref[...])


def _ffn_call(x, mod, wp, *, mod_row, tm):
    bsz, seq, d = x.shape
    halo = SUBLANES
    nblk = seq // halo
    per = tm // halo
    return pl.pallas_call(
        functools.partial(_ffn_kernel, mod_row=mod_row),
        grid=(bsz, seq // tm),
        in_specs=[_const_spec(mod.shape),
                  pl.BlockSpec((None, halo, d), lambda b, i: (b, jnp.maximum(i * per - 1, 0), 0)),
                  pl.BlockSpec((None, tm, d), lambda b, i: (b, i, 0)),
                  pl.BlockSpec((None, halo, d), lambda b, i: (b, jnp.minimum((i + 1) * per, nblk - 1), 0)),
                  _const_spec((1, d)), _const_spec(wp["w_up"].shape), _const_spec(wp["conv_w"].shape),
                  _const_spec(wp["conv_b"].shape), _const_spec(wp["w_down"].shape)],
        out_specs=pl.BlockSpec((None, tm, d), lambda b, i: (b, i, 0)),
        out_shape=jax.ShapeDtypeStruct(x.shape, F32),
        compiler_params=_cparams(2),
        name="ffn",
    )(mod, x, x, x, wp["g_norm2"], wp["w_up"], wp["conv_w"], wp["conv_b"], wp["w_down"])


def _slots(w, n_slots, width, offset=0):
    k = w.shape[0]
    w3 = w.reshape(k, n_slots, width)
    w3 = jnp.pad(w3, ((0, 0), (0, 0), (offset, LANES - offset - width)))
    return w3.reshape(k, n_slots * LANES)


def _prep_layer(l, p):
    d = p["w_in"].shape[1]
    w_in = p["w_in"][l]
    o = 0
    c_kv = w_in[:, o:o + MLA_KV_RANK]; o += MLA_KV_RANK
    k_rope = w_in[:, o:o + MLA_ROPE]; o += MLA_ROPE
    k_g = w_in[:, o:o + GQA_KV_HEADS * GQA_DIM]; o += GQA_KV_HEADS * GQA_DIM
    v_g = w_in[:, o:o + GQA_KV_HEADS * GQA_DIM]; o += GQA_KV_HEADS * GQA_DIM
    c_q = w_in[:, o:o + MLA_Q_RANK]; o += MLA_Q_RANK
    q_g = w_in[:, o:o + GQA_HEADS * GQA_DIM]; o += GQA_HEADS * GQA_DIM
    four = w_in[:, o:o + FOUR_WIDTH]; o += FOUR_WIDTH
    gates = w_in[:, o:o + N_BRANCH * d]
    vg_lo = _slots(v_g, GQA_KV_HEADS, GQA_DIM, 0).reshape(d, GQA_KV_HEADS, 1, LANES)
    vg_hi = _slots(v_g, GQA_KV_HEADS, GQA_DIM, GQA_DIM).reshape(d, GQA_KV_HEADS, 1, LANES)
    w1 = jnp.concatenate([
        c_kv,
        _slots(k_rope, 1, MLA_ROPE, MLA_NOPE),
        _slots(k_g, GQA_KV_HEADS, GQA_DIM),
        jnp.concatenate([vg_lo, vg_hi], axis=2).reshape(d, 2 * GQA_KV_HEADS * LANES),
        c_q,
        _slots(q_g, GQA_HEADS, GQA_DIM),
        four], axis=1).astype(BF16)

    w_ukv = p["w_ukv"][l].reshape(MLA_KV_RANK, MLA_HEADS, MLA_NOPE + MLA_V)
    k_nope = w_ukv[:, :, :MLA_NOPE].reshape(MLA_KV_RANK, MLA_HEADS * MLA_NOPE)
    v_mla = w_ukv[:, :, MLA_NOPE:]
    v_slots = jnp.stack([jnp.pad(v_mla[:, hd], ((0, 0), ((hd % 2) * MLA_V, LANES - MLA_V - (hd % 2) * MLA_V)))
                         for hd in range(MLA_HEADS)], axis=1).reshape(MLA_KV_RANK, MLA_HEADS * LANES)
    w_ukv2 = jnp.concatenate([_slots(k_nope, MLA_HEADS, MLA_NOPE), v_slots], axis=1).astype(BF16)
    w_uq2 = _slots(p["w_uq"][l], MLA_HEADS, MLA_QK).astype(BF16)

    pad = lambda g: jnp.pad(g, (0, LANES - g.shape[0]))
    gslot = jnp.stack([pad(p["g_kn_mla"][l]), pad(p["g_qn_mla"][l]) * (MLA_QK ** -0.5),
                       pad(p["g_kn_gqa"][l]), pad(p["g_qn_gqa"][l]) * (GQA_DIM ** -0.5)])
    gslot = jnp.pad(gslot, ((0, SUBLANES - gslot.shape[0]), (0, 0)))
    return {
        "w1": w1, "w_ukv": w_ukv2, "w_uq": w_uq2, "gslot": gslot,
        "g_norm1": p["g_norm1"][l][None], "g_norm2": p["g_norm2"][l][None],
        "g_ckv": p["g_ckv"][l][None], "g_cq": p["g_cq"][l][None],
        "w_gate": gates.astype(BF16),
        "w_br_mla": p["w_br_mla"][l].astype(BF16), "w_br_gqa": p["w_br_gqa"][l].astype(BF16),
        "w_four": p["w_four"][l].astype(BF16), "w_o": p["w_o"][l].astype(BF16),
        "w_up": p["w_up"][l].astype(BF16), "conv_w": p["conv_w"][l], "conv_b": p["conv_b"][l][None],
        "w_down": p["w_down"][l].astype(BF16),
    }


def _rope_tables(seq, rot_dim, lane0):
    rows = seq // GRID_W
    row = np.repeat(np.arange(rows, dtype=np.float32), GRID_W)
    col = np.tile(np.arange(GRID_W, dtype=np.float32), rows)
    n_f = rot_dim // 4
    inv = (ROPE_BASE ** (-np.arange(n_f, dtype=np.float32) / n_f)).astype(np.float32)
    ang = jnp.asarray(np.concatenate([row[:, None] * inv, col[:, None] * inv], axis=-1), F32)
    cos, sin = jnp.cos(ang), jnp.sin(ang)
    half = rot_dim // 2
    z = lambda n: jnp.zeros((seq, n), F32)
    o = lambda n: jnp.ones((seq, n), F32)
    rest = LANES - lane0 - rot_dim
    return jnp.stack([
        jnp.concatenate([o(lane0), cos, cos, o(rest)], axis=1),
        jnp.concatenate([z(lane0), -sin, z(half), z(rest)], axis=1),
        jnp.concatenate([z(lane0), z(half), sin, z(rest)], axis=1)])


def kernel(x, c, ctx, c_ctx, w_mod, b_mod, g_norm1, g_norm2, w_in, g_cq, g_ckv, w_uq, w_ukv, g_qn_mla, g_kn_mla,
           g_qn_gqa, g_kn_gqa, w_br_mla, w_br_gqa, w_four, w_o, w_up, conv_w, conv_b, w_down):
    params = dict(w_in=w_in, g_norm1=g_norm1, g_norm2=g_norm2, g_cq=g_cq, g_ckv=g_ckv, w_uq=w_uq, w_ukv=w_ukv,
                  g_qn_mla=g_qn_mla, g_kn_mla=g_kn_mla, g_qn_gqa=g_qn_gqa, g_kn_gqa=g_kn_gqa,
                  w_br_mla=w_br_mla, w_br_gqa=w_br_gqa, w_four=w_four, w_o=w_o, w_up=w_up, conv_w=conv_w,
                  conv_b=conv_b, w_down=w_down)
    bsz, seq, d = x.shape
    ctx_len = ctx.shape[1]
    depth = w_mod.shape[0]
    tm = 256
    tm_c = min(tm, ctx_len)

    c_all = jnp.concatenate([c, c_ctx[None], jnp.zeros((SUBLANES - (bsz + 1) % SUBLANES, d), F32)], axis=0)
    mod_all = _mod_call(c_all, w_mod, b_mod)
    tabs = (_rope_tables(seq, MLA_ROPE, MLA_NOPE), _rope_tables(seq, GQA_DIM, 0))
    tabs_c = tuple(t[:, :ctx_len] for t in tabs)

    xc = ctx
    for l in range(depth):
        wp = _prep_layer(l, params)
        mod = mod_all[l]
        update_ctx = l < depth - 1
        km, vm, kg, vg, qm, qg, four = _inproj_call(x, mod, wp, tabs, mod_row=None, rope=True, kv_only=False, tm=tm)
        c_out = _inproj_call(xc, mod, wp, tabs_c, mod_row=bsz, rope=False, kv_only=not update_ctx, tm=tm_c)
        kmc, vmc, kgc, vgc = c_out[:4]
        o_mla = _attn_call(qm, [(kmc, vmc), (km, vm)], shared_kv=False, tq=256, name="attn_mla")
        o_gqa = _attn_call(qg, [(kgc, vgc), (kg, vg)], shared_kv=True, tq=256, name="attn_gqa")
        o_four = _fourier_call(four)
        x = _merge_call(x, mod, o_mla, o_gqa, o_four, wp, mod_row=None, tm=tm)
        x = _ffn_call(x, mod, wp, mod_row=None, tm=tm)
        if update_ctx:
            qmc, qgc, fourc = c_out[4:]
            oc_mla = _attn_call(qmc, [(kmc, vmc)], shared_kv=False, tq=tm_c, name="attn_mla_ctx")
            oc_gqa = _attn_call(qgc, [(kgc, vgc)], shared_kv=True, tq=tm_c, name="attn_gqa_ctx")
            oc_four = _fourier_call(fourc)
            xc = _merge_call(xc, mod, oc_mla, oc_gqa, oc_four, wp, mod_row=bsz, tm=tm_c)
            xc = _ffn_call(xc, mod, wp, mod_row=bsz, tm=tm_c)
    return x
```

```python
import functools

import numpy as np
import jax
import jax.numpy as jnp
from jax import lax
from jax.experimental import pallas as pl
from jax.experimental.pallas import tpu as pltpu

F32 = jnp.float32
BF16 = jnp.bfloat16

LANES = 128
SUBLANES = 8
VMEM_LIMIT = 56 * 1024 * 1024

GRID_W = 64
ROPE_BASE = 10000.0
EPS = 1e-6
MLA_HEADS = 8
MLA_NOPE = 64
MLA_ROPE = 32
MLA_QK = MLA_NOPE + MLA_ROPE
MLA_V = 64
MLA_Q_RANK = 384
MLA_KV_RANK = 256
GQA_HEADS = 8
GQA_KV_HEADS = 2
GQA_DIM = 64
FOUR_GROUPS = 4
FOUR_GROUP_DIM = 128
FOUR_WIDTH = FOUR_GROUPS * FOUR_GROUP_DIM
N_BRANCH = 3
SUM_LANE = 64
LOG2E = 1.4426950408889634
SAFE_SCORE_BOUND = 40.0

C_CKV = 0
C_KROPE = C_CKV + MLA_KV_RANK
C_KG = C_KROPE + LANES
C_VG = C_KG + GQA_KV_HEADS * LANES
C_KV_END = C_VG + GQA_KV_HEADS * LANES
C_CQ = C_KV_END
C_QG = C_CQ + MLA_Q_RANK
C_FOUR = C_QG + GQA_HEADS * LANES
C_END = C_FOUR + FOUR_WIDTH


def _cparams(n_grid, vmem=VMEM_LIMIT):
    return pltpu.CompilerParams(dimension_semantics=("arbitrary",) * n_grid, vmem_limit_bytes=vmem)


def _const_spec(shape):
    nd = len(shape)
    return pl.BlockSpec(shape, lambda *_: (0,) * nd, pipeline_mode=pl.Buffered(1))


def _rms(x, g, dim):
    ss = jnp.sum(x * x, axis=-1, keepdims=True)
    return x * lax.rsqrt(ss * (1.0 / dim) + EPS) * g


def _modulate(x, g, shift, scale):
    return _rms(x, g, x.shape[-1]) * (1.0 + scale) + shift


def _rope_slot(x, tab_ref, half):
    up = pltpu.roll(x, LANES - half, axis=1)
    dn = pltpu.roll(x, half, axis=1)
    return x * tab_ref[0] + up * tab_ref[1] + dn * tab_ref[2]


def _split(x):
    hi = x.astype(BF16)
    lo = (x - hi.astype(F32)).astype(BF16)
    return hi, lo


def _dot(a, b):
    return jnp.dot(a, b, preferred_element_type=F32)


def _dot3_lhs_const(c_ref, x):
    xh, xl = _split(x)
    return _dot(c_ref[0], xh) + _dot(c_ref[1], xh) + _dot(c_ref[0], xl)


def _dot3_rhs_const(x, c_ref):
    xh, xl = _split(x)
    return _dot(xh, c_ref[0]) + _dot(xh, c_ref[1]) + _dot(xl, c_ref[0])


def _mod_kernel(c_ref, w_ref, b_ref, o_ref):
    c = c_ref[...]
    s = (c * (1.0 / (1.0 + jnp.exp(-c)))).astype(BF16)
    o_ref[...] = _dot(s, w_ref[...].astype(BF16)) + b_ref[...]


def _mod_call(c_all, w_mod, b_mod):
    depth, d, n = w_mod.shape
    tn = n // 4
    return pl.pallas_call(
        _mod_kernel,
        grid=(depth, n // tn),
        in_specs=[pl.BlockSpec(c_all.shape, lambda l, j: (0, 0)),
                  pl.BlockSpec((None, d, tn), lambda l, j: (l, 0, j)),
                  pl.BlockSpec((None, 1, tn), lambda l, j: (l, 0, j))],
        out_specs=pl.BlockSpec((None, c_all.shape[0], tn), lambda l, j: (l, 0, j)),
        out_shape=jax.ShapeDtypeStruct((depth, c_all.shape[0], n), F32),
        compiler_params=_cparams(2),
        name="modulation",
    )(c_all, w_mod, b_mod.reshape(depth, 1, n))


(G_KN_MLA, G_QN_MLA, G_KN_GQA, G_QN_GQA, AUG_Q_MLA, AUG_K_MLA, AUG_Q_GQA, AUG_K_GQA, AUG_V) = range(9)
GSLOT_ROWS = 2 * SUBLANES


def _head_slot(x, gslot_ref, g_row, aug_row, dim, tab_ref, half):
    xh = _rms(x, gslot_ref[g_row:g_row + 1, :], dim)
    if tab_ref is not None:
        xh = _rope_slot(xh, tab_ref, half)
    return xh + gslot_ref[aug_row:aug_row + 1, :]


def _inproj_kernel(mod_ref, x_ref, g1_ref, w1_ref, gckv_ref, gcq_ref, wukv_ref, wuq_ref, gslot_ref,
                   rm_ref, rg_ref, *out_refs, mod_row, rope, kv_only):
    d = x_ref.shape[-1]
    b = pl.program_id(0) if mod_row is None else mod_row
    shift = mod_ref[pl.ds(b, 1), 0:d]
    scale = mod_ref[pl.ds(b, 1), d:2 * d]
    h = _modulate(x_ref[...], g1_ref[...], shift, scale).astype(BF16)
    p = _dot(h, w1_ref[...])
    km_ref, vm_ref, kg_ref, vg_ref = out_refs[:4]
    rm = rm_ref if rope else None
    rg = rg_ref if rope else None
    aug_v = gslot_ref[AUG_V:AUG_V + 1, :]

    ckv = _rms(p[:, C_CKV:C_CKV + MLA_KV_RANK], gckv_ref[...], MLA_KV_RANK).astype(BF16)
    kv = _dot(ckv, wukv_ref[...])
    krope = p[:, C_KROPE:C_KROPE + LANES]
    for hd in range(MLA_HEADS):
        sl = slice(hd * LANES, (hd + 1) * LANES)
        kh = _head_slot(kv[:, sl] + krope, gslot_ref, G_KN_MLA, AUG_K_MLA, MLA_QK, rm, MLA_ROPE // 2)
        km_ref[:, sl] = kh.astype(km_ref.dtype)
        vm_ref[:, sl] = (kv[:, MLA_HEADS * LANES + hd * LANES:MLA_HEADS * LANES + (hd + 1) * LANES]
                         + aug_v).astype(vm_ref.dtype)

    for hd in range(GQA_KV_HEADS):
        sl = slice(hd * LANES, (hd + 1) * LANES)
        kh = _head_slot(p[:, C_KG + hd * LANES:C_KG + (hd + 1) * LANES], gslot_ref, G_KN_GQA, AUG_K_GQA,
                        GQA_DIM, rg, GQA_DIM // 2)
        kg_ref[:, sl] = kh.astype(kg_ref.dtype)
        vg_ref[:, sl] = (p[:, C_VG + hd * LANES:C_VG + (hd + 1) * LANES] + aug_v).astype(vg_ref.dtype)
    if kv_only:
        return

    qm_ref, qg_ref, four_ref = out_refs[4:]
    cq = _rms(p[:, C_CQ:C_CQ + MLA_Q_RANK], gcq_ref[...], MLA_Q_RANK).astype(BF16)
    qm = _dot(cq, wuq_ref[...])
    for hd in range(MLA_HEADS):
        sl = slice(hd * LANES, (hd + 1) * LANES)
        qh = _head_slot(qm[:, sl], gslot_ref, G_QN_MLA, AUG_Q_MLA, MLA_QK, rm, MLA_ROPE // 2)
        qm_ref[:, sl] = qh.astype(qm_ref.dtype)
    for hd in range(GQA_HEADS):
        sl = slice(hd * LANES, (hd + 1) * LANES)
        qh = _head_slot(p[:, C_QG + hd * LANES:C_QG + (hd + 1) * LANES], gslot_ref, G_QN_GQA, AUG_Q_GQA,
                        GQA_DIM, rg, GQA_DIM // 2)
        qg_ref[:, sl] = qh.astype(qg_ref.dtype)
    four_ref[...] = p[:, C_FOUR:C_END]


def _inproj_call(x, mod, wp, tabs, *, mod_row, rope, kv_only, tm):
    bsz, seq, d = x.shape
    ncol = C_KV_END if kv_only else C_END
    kernel = functools.partial(_inproj_kernel, mod_row=mod_row, rope=rope, kv_only=kv_only)
    row = lambda width, dt: (pl.BlockSpec((None, tm, width), lambda b, i: (b, i, 0)),
                             jax.ShapeDtypeStruct((bsz, seq, width), dt))
    outs = [row(MLA_HEADS * LANES, BF16), row(MLA_HEADS * LANES, BF16),
            row(GQA_KV_HEADS * LANES, BF16), row(GQA_KV_HEADS * LANES, BF16)]
    if not kv_only:
        outs += [row(MLA_HEADS * LANES, BF16), row(GQA_HEADS * LANES, BF16), row(FOUR_WIDTH, F32)]
    tab_spec = pl.BlockSpec((3, tm, LANES), lambda b, i: (0, i, 0))
    return pl.pallas_call(
        kernel,
        grid=(bsz, seq // tm),
        in_specs=[_const_spec(mod.shape),
                  pl.BlockSpec((None, tm, d), lambda b, i: (b, i, 0)),
                  _const_spec((1, d)),
                  pl.BlockSpec((d, ncol), lambda b, i: (0, 0), pipeline_mode=pl.Buffered(1)),
                  _const_spec((1, MLA_KV_RANK)), _const_spec((1, MLA_Q_RANK)),
                  _const_spec(wp["w_ukv"].shape), _const_spec(wp["w_uq"].shape),
                  _const_spec(wp["gslot"].shape), tab_spec, tab_spec],
        out_specs=[o[0] for o in outs],
        out_shape=[o[1] for o in outs],
        compiler_params=_cparams(2),
        name="inproj_kv" if kv_only else "inproj",
    )(mod, x, wp["g_norm1"], wp["w1"], wp["g_ckv"], wp["g_cq"], wp["w_ukv"], wp["w_uq"], wp["gslot"],
      tabs[0], tabs[1])


def _attn_kernel(q_ref, *refs, n_src, n_heads, shared_kv, bounded):
    kv_refs, o_ref = refs[:2 * n_src], refs[2 * n_src]
    nt = (((1,), (1,)), ((), ()))
    tq = q_ref.shape[0]
    low_half = lax.broadcasted_iota(jnp.int32, (tq, LANES), 1) < SUM_LANE
    for pair in range(n_heads // 2):
        halves = []
        for hh in range(2):
            i = pair * 2 + hh
            q = q_ref[:, i * LANES:(i + 1) * LANES]
            col = 0 if shared_kv else i * LANES
            ss = [lax.dot_general(q, kv_refs[2 * s][:, col:col + LANES], nt, preferred_element_type=F32)
                  for s in range(n_src)]
            if not bounded:
                m = functools.reduce(jnp.maximum, [jnp.max(s, axis=-1, keepdims=True) for s in ss])
                ss = [s - m for s in ss]
            o = functools.reduce(jnp.add, [_dot(jnp.exp2(ss[s]).astype(BF16), kv_refs[2 * s + 1][:, col:col + LANES])
                                           for s in range(n_src)])
            halves.append(o / o[:, SUM_LANE:SUM_LANE + 1])
        out = jnp.where(low_half, halves[0], pltpu.roll(halves[1], SUM_LANE, axis=1))
        o_ref[:, pair * LANES:(pair + 1) * LANES] = out.astype(o_ref.dtype)


def _attn_call(q, srcs, safe, *, shared_kv, tq, name):
    bsz, lq, qw = q.shape
    n_heads = 4 if shared_kv else 2
    n_groups = qw // (n_heads * LANES)
    kw = LANES if shared_kv else 2 * LANES
    in_specs = [pl.BlockSpec((None, tq, n_heads * LANES), lambda b, g, i: (b, i, g))]
    args = [q]
    for k, v in srcs:
        lk = k.shape[1]
        in_specs += [pl.BlockSpec((None, lk, kw), lambda b, g, i: (b, 0, g)),
                     pl.BlockSpec((None, lk, kw), lambda b, g, i: (b, 0, g))]
        args += [k, v]
    ow = n_heads // 2 * LANES

    def call(bounded):
        return pl.pallas_call(
            functools.partial(_attn_kernel, n_src=len(srcs), n_heads=n_heads, shared_kv=shared_kv, bounded=bounded),
            grid=(bsz, n_groups, lq // tq),
            in_specs=in_specs,
            out_specs=pl.BlockSpec((None, tq, ow), lambda b, g, i: (b, i, g)),
            out_shape=jax.ShapeDtypeStruct((bsz, lq, n_groups * ow), BF16),
            compiler_params=_cparams(3),
            name=name + ("" if bounded else "_rowmax"),
        )

    return lax.cond(safe, lambda *a: call(True)(*a), lambda *a: call(False)(*a), *args)


def _dft_rows_kernel(fa_ref, x_ref, y_ref):
    y_ref[...] = _dot3_lhs_const(fa_ref, x_ref[...])


def _dft_cols_kernel(fb_ref, cs_ref, tc_ref, ts_ref, y_ref, o_ref):
    nr = y_ref.shape[1]
    reps = FOUR_WIDTH // LANES
    for r in range(nr):
        yr, yi = y_ref[0, r], y_ref[1, r]
        tc = jnp.concatenate([tc_ref[r]] * reps, axis=1)
        ts = jnp.concatenate([ts_ref[r]] * reps, axis=1)
        z = jnp.concatenate([yr * tc - yi * ts, yr * ts + yi * tc], axis=0)
        o2 = _dot3_lhs_const(fb_ref, z)
        half = o2.shape[0] // 2
        ocat = jnp.concatenate([o2[:half], o2[half:]], axis=1)
        o_ref[:, r * FOUR_WIDTH:(r + 1) * FOUR_WIDTH] = _dot3_rhs_const(ocat, cs_ref)


def _dft_small_kernel(f_ref, cs_ref, x_ref, o_ref):
    x2 = _dot3_lhs_const(f_ref, x_ref[...])
    half = x2.shape[0] // 2
    ocat = jnp.concatenate([x2[:half], x2[half:]], axis=1)
    o_ref[...] = _dot3_rhs_const(ocat, cs_ref)


def _np_split(a):
    a = np.asarray(a, np.float32)
    hi = jnp.asarray(a, F32).astype(BF16)
    lo = (jnp.asarray(a, F32) - hi.astype(F32)).astype(BF16)
    return jnp.stack([hi, lo])


def _dft_cos_sin(n):
    idx = np.arange(n)
    ang = 2.0 * np.pi * ((idx[:, None] * idx[None, :]) % n) / n
    return np.cos(ang), np.sin(ang)


def _channel_dft_const(seq):
    c, s = _dft_cos_sin(FOUR_GROUP_DIM)
    eye = np.eye(FOUR_GROUPS)
    norm = 1.0 / np.sqrt(seq * FOUR_GROUP_DIM)
    return _np_split(np.concatenate([np.kron(eye, c), np.kron(eye, s)], axis=0) * norm)


def _fourier_call(f):
    bsz, seq, w = f.shape
    cs = _channel_dft_const(seq)
    if seq < GRID_W * SUBLANES:
        c, s = _dft_cos_sin(seq)
        fmat = _np_split(np.concatenate([c, -s], axis=0))
        return pl.pallas_call(
            _dft_small_kernel,
            grid=(bsz,),
            in_specs=[_const_spec(fmat.shape), _const_spec(cs.shape),
                      pl.BlockSpec((None, seq, w), lambda b: (b, 0, 0))],
            out_specs=pl.BlockSpec((None, seq, w), lambda b: (b, 0, 0)),
            out_shape=jax.ShapeDtypeStruct((bsz, seq, w), F32),
            compiler_params=_cparams(1),
            name="dft_small",
        )(fmat, cs, f)

    rows = seq // GRID_W
    ca, sa = _dft_cos_sin(rows)
    fa = _np_split(np.concatenate([ca, -sa], axis=0))
    cb, sb = _dft_cos_sin(GRID_W)
    fb = _np_split(np.block([[cb, sb], [-sb, cb]]))
    ang = 2.0 * np.pi * (np.arange(rows)[:, None] * np.arange(GRID_W)[None, :]) / seq
    tc = jnp.asarray(np.repeat(np.cos(ang)[:, :, None], LANES, axis=2), F32)
    ts = jnp.asarray(np.repeat(-np.sin(ang)[:, :, None], LANES, axis=2), F32)

    nb = SUBLANES
    y = pl.pallas_call(
        _dft_rows_kernel,
        grid=(bsz, GRID_W // nb),
        in_specs=[_const_spec(fa.shape),
                  pl.BlockSpec((None, rows, nb * w), lambda b, j: (b, 0, j))],
        out_specs=pl.BlockSpec((None, 2 * rows, nb * w), lambda b, j: (b, 0, j)),
        out_shape=jax.ShapeDtypeStruct((bsz, 2 * rows, GRID_W * w), F32),
        compiler_params=_cparams(2),
        name="dft_rows",
    )(fa, f.reshape(bsz, rows, GRID_W * w))

    nr = SUBLANES
    o = pl.pallas_call(
        _dft_cols_kernel,
        grid=(bsz, rows // nr),
        in_specs=[_const_spec(fb.shape), _const_spec(cs.shape),
                  pl.BlockSpec((nr, GRID_W, LANES), lambda b, j: (j, 0, 0)),
                  pl.BlockSpec((nr, GRID_W, LANES), lambda b, j: (j, 0, 0)),
                  pl.BlockSpec((None, 2, nr, GRID_W, w), lambda b, j: (b, 0, j, 0, 0))],
        out_specs=pl.BlockSpec((None, GRID_W, nr * w), lambda b, j: (b, 0, j)),
        out_shape=jax.ShapeDtypeStruct((bsz, GRID_W, rows * w), F32),
        compiler_params=_cparams(2),
        name="dft_cols",
    )(fb, cs, tc, ts, y.reshape(bsz, 2, rows, GRID_W, w))
    return o.reshape(bsz, seq, w)


def _merge_kernel(mod_ref, x_ref, g1_ref, om_ref, og_ref, of_ref, wg_ref, wbm_ref, wbg_ref, wbf_ref, wo_ref,
                  o_ref, *, mod_row):
    d = x_ref.shape[-1]
    b = pl.program_id(0) if mod_row is None else mod_row
    shift = mod_ref[pl.ds(b, 1), 0:d]
    scale = mod_ref[pl.ds(b, 1), d:2 * d]
    gate1 = mod_ref[pl.ds(b, 1), 2 * d:3 * d]
    x = x_ref[...]
    h = _modulate(x, g1_ref[...], shift, scale).astype(BF16)
    y = None
    branches = ((om_ref[...], wbm_ref), (og_ref[...], wbg_ref), (of_ref[...].astype(BF16), wbf_ref))
    for n, (o, w_ref) in enumerate(branches):
        gate = 1.0 / (1.0 + jnp.exp(-_dot(h, wg_ref[:, n * d:(n + 1) * d])))
        t = gate * _dot(o, w_ref[...])
        y = t if y is None else y + t
    o_ref[...] = x + gate1 * _dot(y.astype(BF16), wo_ref[...])


def _merge_call(x, mod, o_mla, o_gqa, o_four, wp, *, mod_row, tm):
    bsz, seq, d = x.shape
    row = lambda width: pl.BlockSpec((None, tm, width), lambda b, i: (b, i, 0))
    return pl.pallas_call(
        functools.partial(_merge_kernel, mod_row=mod_row),
        grid=(bsz, seq // tm),
        in_specs=[_const_spec(mod.shape), row(d), _const_spec((1, d)),
                  row(o_mla.shape[-1]), row(o_gqa.shape[-1]), row(o_four.shape[-1]),
                  _const_spec(wp["w_gate"].shape), _const_spec(wp["w_br_mla"].shape),
                  _const_spec(wp["w_br_gqa"].shape), _const_spec(wp["w_four"].shape),
                  _const_spec(wp["w_o"].shape)],
        out_specs=row(d),
        out_shape=jax.ShapeDtypeStruct(x.shape, F32),
        compiler_params=_cparams(2),
        name="merge",
    )(mod, x, wp["g_norm1"], o_mla, o_gqa, o_four, wp["w_gate"], wp["w_br_mla"], wp["w_br_gqa"],
      wp["w_four"], wp["w_o"])


def _ffn_kernel(mod_ref, xp_ref, x_ref, xn_ref, g2_ref, wup_ref, cw_ref, cb_ref, wdn_ref, o_ref, *, mod_row):
    d = x_ref.shape[-1]
    tm = x_ref.shape[0]
    halo = xp_ref.shape[0]
    b = pl.program_id(0) if mod_row is None else mod_row
    i = pl.program_id(1)
    shift = mod_ref[pl.ds(b, 1), 3 * d:4 * d]
    scale = mod_ref[pl.ds(b, 1), 4 * d:5 * d]
    gate2 = mod_ref[pl.ds(b, 1), 5 * d:6 * d]
    x = x_ref[...]
    xe = jnp.concatenate([xp_ref[...], x, xn_ref[...]], axis=0)
    he = _modulate(xe, g2_ref[...], shift, scale)
    rid = lax.broadcasted_iota(jnp.int32, (tm + 2 * halo, 1), 0)
    outside = ((rid < halo) & (i == 0)) | ((rid >= tm + halo) & (i == pl.num_programs(1) - 1))
    he = jnp.where(outside, 0.0, he).astype(BF16)
    u = _dot(he, wup_ref[...])
    n = tm + 2 * halo
    up = pltpu.roll(u, 1, axis=0)[halo:halo + tm]
    dn = pltpu.roll(u, n - 1, axis=0)[halo:halo + tm]
    u = cw_ref[0:1, :] * up + cw_ref[1:2, :] * u[halo:halo + tm] + cw_ref[2:3, :] * dn + cb_ref[...]
    ff = u.shape[-1] // 2
    a, v = u[:, :ff], u[:, ff:]
    act = (a * (1.0 / (1.0 + jnp.exp(-a))) * v).astype(BF16)
    o_ref[...] = x + gate2 * _dot(act, wdn_ref[...])


def _ffn_call(x, mod, wp, *, mod_row, tm):
    bsz, seq, d = x.shape
    halo = SUBLANES
    nblk = seq // halo
    per = tm // halo
    return pl.pallas_call(
        functools.partial(_ffn_kernel, mod_row=mod_row),
        grid=(bsz, seq // tm),
        in_specs=[_const_spec(mod.shape),
                  pl.BlockSpec((None, halo, d), lambda b, i: (b, jnp.maximum(i * per - 1, 0), 0)),
                  pl.BlockSpec((None, tm, d), lambda b, i: (b, i, 0)),
                  pl.BlockSpec((None, halo, d), lambda b, i: (b, jnp.minimum((i + 1) * per, nblk - 1), 0)),
                  _const_spec((1, d)), _const_spec(wp["w_up"].shape), _const_spec(wp["conv_w"].shape),
                  _const_spec(wp["conv_b"].shape), _const_spec(wp["w_down"].shape)],
        out_specs=pl.BlockSpec((None, tm, d), lambda b, i: (b, i, 0)),
        out_shape=jax.ShapeDtypeStruct(x.shape, F32),
        compiler_params=_cparams(2),
        name="ffn",
    )(mod, x, x, x, wp["g_norm2"], wp["w_up"], wp["conv_w"], wp["conv_b"], wp["w_down"])


def _slots(w, n_slots, width, offset=0):
    k = w.shape[0]
    w3 = w.reshape(k, n_slots, width)
    w3 = jnp.pad(w3, ((0, 0), (0, 0), (offset, LANES - offset - width)))
    return w3.reshape(k, n_slots * LANES)


def _prep_layer(l, p):
    d = p["w_in"].shape[1]
    w_in = p["w_in"][l]
    o = 0
    c_kv = w_in[:, o:o + MLA_KV_RANK]; o += MLA_KV_RANK
    k_rope = w_in[:, o:o + MLA_ROPE]; o += MLA_ROPE
    k_g = w_in[:, o:o + GQA_KV_HEADS * GQA_DIM]; o += GQA_KV_HEADS * GQA_DIM
    v_g = w_in[:, o:o + GQA_KV_HEADS * GQA_DIM]; o += GQA_KV_HEADS * GQA_DIM
    c_q = w_in[:, o:o + MLA_Q_RANK]; o += MLA_Q_RANK
    q_g = w_in[:, o:o + GQA_HEADS * GQA_DIM]; o += GQA_HEADS * GQA_DIM
    four = w_in[:, o:o + FOUR_WIDTH]; o += FOUR_WIDTH
    gates = w_in[:, o:o + N_BRANCH * d]
    w1 = jnp.concatenate([
        c_kv,
        _slots(k_rope, 1, MLA_ROPE, MLA_NOPE),
        _slots(k_g, GQA_KV_HEADS, GQA_DIM),
        _slots(v_g, GQA_KV_HEADS, GQA_DIM),
        c_q,
        _slots(q_g, GQA_HEADS, GQA_DIM),
        four], axis=1).astype(BF16)

    w_ukv = p["w_ukv"][l].reshape(MLA_KV_RANK, MLA_HEADS, MLA_NOPE + MLA_V)
    k_nope = w_ukv[:, :, :MLA_NOPE].reshape(MLA_KV_RANK, MLA_HEADS * MLA_NOPE)
    v_mla = w_ukv[:, :, MLA_NOPE:].reshape(MLA_KV_RANK, MLA_HEADS * MLA_V)
    w_ukv2 = jnp.concatenate([_slots(k_nope, MLA_HEADS, MLA_NOPE), _slots(v_mla, MLA_HEADS, MLA_V)],
                             axis=1).astype(BF16)
    w_uq2 = _slots(p["w_uq"][l], MLA_HEADS, MLA_QK).astype(BF16)

    pad = lambda g: jnp.pad(g, (0, LANES - g.shape[0]))
    onehot = lambda lane: jnp.zeros((LANES,), F32).at[lane].set(1.0)
    bound_m = (MLA_QK ** 0.5) * jnp.max(jnp.abs(p["g_qn_mla"][l])) * jnp.max(jnp.abs(p["g_kn_mla"][l])) * (1 + 2.0 ** -6)
    bound_g = (GQA_DIM ** 0.5) * jnp.max(jnp.abs(p["g_qn_gqa"][l])) * jnp.max(jnp.abs(p["g_kn_gqa"][l])) * (1 + 2.0 ** -6)
    safe_m, safe_g = bound_m <= SAFE_SCORE_BOUND, bound_g <= SAFE_SCORE_BOUND
    rows = [None] * 9
    rows[G_KN_MLA] = pad(p["g_kn_mla"][l])
    rows[G_QN_MLA] = pad(p["g_qn_mla"][l]) * (MLA_QK ** -0.5 * LOG2E)
    rows[G_KN_GQA] = pad(p["g_kn_gqa"][l])
    rows[G_QN_GQA] = pad(p["g_qn_gqa"][l]) * (GQA_DIM ** -0.5 * LOG2E)
    rows[AUG_Q_MLA] = onehot(MLA_QK)
    rows[AUG_K_MLA] = onehot(MLA_QK) * jnp.where(safe_m, -bound_m * LOG2E, 0.0)
    rows[AUG_Q_GQA] = onehot(GQA_DIM)
    rows[AUG_K_GQA] = onehot(GQA_DIM) * jnp.where(safe_g, -bound_g * LOG2E, 0.0)
    rows[AUG_V] = onehot(SUM_LANE)
    gslot = jnp.stack(rows)
    gslot = jnp.pad(gslot, ((0, GSLOT_ROWS - gslot.shape[0]), (0, 0)))
    return {
        "w1": w1, "w_ukv": w_ukv2, "w_uq": w_uq2, "gslot": gslot, "safe_mla": safe_m, "safe_gqa": safe_g,
        "g_norm1": p["g_norm1"][l][None], "g_norm2": p["g_norm2"][l][None],
        "g_ckv": p["g_ckv"][l][None], "g_cq": p["g_cq"][l][None],
        "w_gate": gates.astype(BF16),
        "w_br_mla": p["w_br_mla"][l].astype(BF16), "w_br_gqa": p["w_br_gqa"][l].astype(BF16),
        "w_four": p["w_four"][l].astype(BF16), "w_o": p["w_o"][l].astype(BF16),
        "w_up": p["w_up"][l].astype(BF16), "conv_w": p["conv_w"][l], "conv_b": p["conv_b"][l][None],
        "w_down": p["w_down"][l].astype(BF16),
    }


def _rope_tables(seq, rot_dim, lane0):
    rows = seq // GRID_W
    row = np.repeat(np.arange(rows, dtype=np.float32), GRID_W)
    col = np.tile(np.arange(GRID_W, dtype=np.float32), rows)
    n_f = rot_dim // 4
    inv = (ROPE_BASE ** (-np.arange(n_f, dtype=np.float32) / n_f)).astype(np.float32)
    ang = jnp.asarray(np.concatenate([row[:, None] * inv, col[:, None] * inv], axis=-1), F32)
    cos, sin = jnp.cos(ang), jnp.sin(ang)
    half = rot_dim // 2
    z = lambda n: jnp.zeros((seq, n), F32)
    o = lambda n: jnp.ones((seq, n), F32)
    rest = LANES - lane0 - rot_dim
    return jnp.stack([
        jnp.concatenate([o(lane0), cos, cos, o(rest)], axis=1),
        jnp.concatenate([z(lane0), -sin, z(half), z(rest)], axis=1),
        jnp.concatenate([z(lane0), z(half), sin, z(rest)], axis=1)])


def kernel(x, c, ctx, c_ctx, w_mod, b_mod, g_norm1, g_norm2, w_in, g_cq, g_ckv, w_uq, w_ukv, g_qn_mla, g_kn_mla,
           g_qn_gqa, g_kn_gqa, w_br_mla, w_br_gqa, w_four, w_o, w_up, conv_w, conv_b, w_down):
    params = dict(w_in=w_in, g_norm1=g_norm1, g_norm2=g_norm2, g_cq=g_cq, g_ckv=g_ckv, w_uq=w_uq, w_ukv=w_ukv,
                  g_qn_mla=g_qn_mla, g_kn_mla=g_kn_mla, g_qn_gqa=g_qn_gqa, g_kn_gqa=g_kn_gqa,
                  w_br_mla=w_br_mla, w_br_gqa=w_br_gqa, w_four=w_four, w_o=w_o, w_up=w_up, conv_w=conv_w,
                  conv_b=conv_b, w_down=w_down)
    bsz, seq, d = x.shape
    ctx_len = ctx.shape[1]
    depth = w_mod.shape[0]
    tm = 256
    tm_c = min(tm, ctx_len)
    tq = 256

    c_all = jnp.concatenate([c, c_ctx[None], jnp.zeros((SUBLANES - (bsz + 1) % SUBLANES, d), F32)], axis=0)
    mod_all = _mod_call(c_all, w_mod, b_mod)
    tabs = (_rope_tables(seq, MLA_ROPE, MLA_NOPE), _rope_tables(seq, GQA_DIM, 0))
    tabs_c = tuple(t[:, :ctx_len] for t in tabs)

    xc = ctx
    for l in range(depth):
        wp = _prep_layer(l, params)
        mod = mod_all[l]
        update_ctx = l < depth - 1
        km, vm, kg, vg, qm, qg, four = _inproj_call(x, mod, wp, tabs, mod_row=None, rope=True, kv_only=False, tm=tm)
        c_out = _inproj_call(xc, mod, wp, tabs_c, mod_row=bsz, rope=False, kv_only=not update_ctx, tm=tm_c)
        kmc, vmc, kgc, vgc = c_out[:4]
        o_mla = _attn_call(qm, [(kmc, vmc), (km, vm)], wp["safe_mla"], shared_kv=False, tq=tq, name="attn_mla")
        o_gqa = _attn_call(qg, [(kgc, vgc), (kg, vg)], wp["safe_gqa"], shared_kv=True, tq=tq, name="attn_gqa")
        o_four = _fourier_call(four)
        x = _merge_call(x, mod, o_mla, o_gqa, o_four, wp, mod_row=None, tm=tm)
        x = _ffn_call(x, mod, wp, mod_row=None, tm=tm)
        if update_ctx:
            qmc, qgc, fourc = c_out[4:]
            oc_mla = _attn_call(qmc, [(kmc, vmc)], wp["safe_mla"], shared_kv=False, tq=tm_c, name="attn_mla_ctx")
            oc_gqa = _attn_call(qgc, [(kgc, vgc)], wp["safe_gqa"], shared_kv=True, tq=tm_c, name="attn_gqa_ctx")
            oc_four = _fourier_call(fourc)
            xc = _merge_call(xc, mod, oc_mla, oc_gqa, oc_four, wp, mod_row=bsz, tm=tm_c)
            xc = _ffn_call(xc, mod, wp, mod_row=bsz, tm=tm_c)
    return x
```

```python
import functools

import numpy as np
import jax
import jax.numpy as jnp
from jax import lax
from jax.experimental import pallas as pl
from jax.experimental.pallas import tpu as pltpu

F32 = jnp.float32
BF16 = jnp.bfloat16

LANES = 128
SUBLANES = 8
VMEM_LIMIT = 56 * 1024 * 1024

GRID_W = 64
ROPE_BASE = 10000.0
EPS = 1e-6
MLA_HEADS = 8
MLA_NOPE = 64
MLA_ROPE = 32
MLA_QK = MLA_NOPE + MLA_ROPE
MLA_V = 64
MLA_Q_RANK = 384
MLA_KV_RANK = 256
GQA_HEADS = 8
GQA_KV_HEADS = 2
GQA_DIM = 64
FOUR_GROUPS = 4
FOUR_GROUP_DIM = 128
FOUR_WIDTH = FOUR_GROUPS * FOUR_GROUP_DIM
N_BRANCH = 3
SUM_LANE = 64
LOG2E = 1.4426950408889634
SAFE_SCORE_BOUND = 40.0
HEADS_PER_STEP = GQA_HEADS // GQA_KV_HEADS

C_CKV = 0
C_KROPE = C_CKV + MLA_KV_RANK
C_KG = C_KROPE + LANES
C_VG = C_KG + GQA_KV_HEADS * LANES
C_KV_END = C_VG + GQA_KV_HEADS * LANES
C_CQ = C_KV_END
C_QG = C_CQ + MLA_Q_RANK
C_FOUR = C_QG + GQA_HEADS * LANES
C_END = C_FOUR + FOUR_WIDTH


def _cparams(n_grid, vmem=VMEM_LIMIT):
    return pltpu.CompilerParams(dimension_semantics=("arbitrary",) * n_grid, vmem_limit_bytes=vmem)


def _const_spec(shape):
    nd = len(shape)
    return pl.BlockSpec(shape, lambda *_: (0,) * nd, pipeline_mode=pl.Buffered(1))


def _rms(x, g, dim):
    ss = jnp.sum(x * x, axis=-1, keepdims=True)
    return x * lax.rsqrt(ss * (1.0 / dim) + EPS) * g


def _modulate(x, g, shift, scale):
    return _rms(x, g, x.shape[-1]) * (1.0 + scale) + shift


def _rope_slot(x, tab_ref, half):
    up = pltpu.roll(x, LANES - half, axis=1)
    dn = pltpu.roll(x, half, axis=1)
    return x * tab_ref[0] + up * tab_ref[1] + dn * tab_ref[2]


def _dot(a, b):
    return jnp.dot(a, b, preferred_element_type=F32)


def _mod_kernel(c_ref, w_ref, b_ref, o_ref):
    c = c_ref[...]
    s = (c * (1.0 / (1.0 + jnp.exp(-c)))).astype(BF16)
    o_ref[...] = _dot(s, w_ref[...].astype(BF16)) + b_ref[...]


def _mod_call(c_all, w_mod, b_mod):
    depth, d, n = w_mod.shape
    tn = n // 4
    return pl.pallas_call(
        _mod_kernel,
        grid=(depth, n // tn),
        in_specs=[pl.BlockSpec(c_all.shape, lambda l, j: (0, 0)),
                  pl.BlockSpec((None, d, tn), lambda l, j: (l, 0, j)),
                  pl.BlockSpec((None, 1, tn), lambda l, j: (l, 0, j))],
        out_specs=pl.BlockSpec((None, c_all.shape[0], tn), lambda l, j: (l, 0, j)),
        out_shape=jax.ShapeDtypeStruct((depth, c_all.shape[0], n), F32),
        compiler_params=_cparams(2),
        name="modulation",
    )(c_all, w_mod, b_mod.reshape(depth, 1, n))


(G_KN_MLA, G_QN_MLA, G_QN_MLA_SW, G_KN_GQA, G_KN_GQA_SW, G_QN_GQA, G_QN_GQA_SW,
 AUG_Q_MLA, AUG_K_MLA, AUG_Q_GQA, AUG_K_GQA, AUG_V) = range(12)
GSLOT_ROWS = 2 * SUBLANES


def _row(ref, r):
    return ref[r:r + 1, :]


def _inv_rms(ss, dim):
    return lax.rsqrt(ss * (1.0 / dim) + EPS)


def _sumsq(x):
    return jnp.sum(x * x, axis=-1, keepdims=True)


def _inproj_kernel(mod_ref, x_ref, g1_ref, w1_ref, gckv_ref, gcq_ref, wukv_ref, wuq_ref, gslot_ref,
                   rm_ref, rg_ref, *out_refs, mod_row, rope, kv_only):
    d = x_ref.shape[-1]
    b = pl.program_id(0) if mod_row is None else mod_row
    shift = mod_ref[pl.ds(b, 1), 0:d]
    scale = mod_ref[pl.ds(b, 1), d:2 * d]
    h = _modulate(x_ref[...], g1_ref[...], shift, scale).astype(BF16)
    p = _dot(h, w1_ref[...])
    km_ref, vm_ref, kg_ref, vg_ref = out_refs[:4]
    aug_v = _row(gslot_ref, AUG_V)

    def gqa_slot(x, g_row, gsw_row, aug_row):
        r = _inv_rms(_sumsq(x), 2 * GQA_DIM)
        if rope:
            y = x * (_row(gslot_ref, g_row) * rg_ref[0]) \
                + pltpu.roll(x, GQA_DIM // 2, axis=1) * (_row(gslot_ref, gsw_row) * rg_ref[1])
        else:
            y = x * _row(gslot_ref, g_row)
        return y * r + _row(gslot_ref, aug_row)

    ckv = _rms(p[:, C_CKV:C_CKV + MLA_KV_RANK], gckv_ref[...], MLA_KV_RANK).astype(BF16)
    kv = _dot(ckv, wukv_ref[...])
    krope = p[:, C_KROPE:C_KROPE + LANES]
    gk = _row(gslot_ref, G_KN_MLA)
    kr = krope * gk
    if rope:
        kr = _rope_slot(kr, rm_ref, MLA_ROPE // 2)
    ss_r = _sumsq(krope)
    for hd in range(MLA_HEADS):
        sl = slice(hd * LANES, (hd + 1) * LANES)
        kh = kv[:, sl]
        r = _inv_rms(_sumsq(kh) + ss_r, MLA_QK)
        km_ref[:, sl] = ((kh * gk + kr) * r + _row(gslot_ref, AUG_K_MLA)).astype(km_ref.dtype)
        vm_ref[:, sl] = (kv[:, MLA_HEADS * LANES + hd * LANES:MLA_HEADS * LANES + (hd + 1) * LANES]
                         + aug_v).astype(vm_ref.dtype)

    for hd in range(GQA_KV_HEADS):
        sl = slice(hd * LANES, (hd + 1) * LANES)
        kh = gqa_slot(p[:, C_KG + hd * LANES:C_KG + (hd + 1) * LANES], G_KN_GQA, G_KN_GQA_SW, AUG_K_GQA)
        kg_ref[:, sl] = kh.astype(kg_ref.dtype)
        vg_ref[:, sl] = (p[:, C_VG + hd * LANES:C_VG + (hd + 1) * LANES] + aug_v).astype(vg_ref.dtype)
    if kv_only:
        return

    qm_ref, qg_ref, four_ref = out_refs[4:]
    cq = _rms(p[:, C_CQ:C_CQ + MLA_Q_RANK], gcq_ref[...], MLA_Q_RANK).astype(BF16)
    qm = _dot(cq, wuq_ref[...])
    a_cos = _row(gslot_ref, G_QN_MLA)
    if rope:
        a_cos = a_cos * rm_ref[0]
        a_sin = _row(gslot_ref, G_QN_MLA_SW) * (rm_ref[2] - rm_ref[1])
    for hd in range(MLA_HEADS):
        sl = slice(hd * LANES, (hd + 1) * LANES)
        q = qm[:, sl]
        y = q * a_cos
        if rope:
            y = y + qm[:, MLA_HEADS * LANES + hd * LANES:MLA_HEADS * LANES + (hd + 1) * LANES] * a_sin
        r = _inv_rms(_sumsq(q), MLA_QK)
        qm_ref[:, sl] = (y * r + _row(gslot_ref, AUG_Q_MLA)).astype(qm_ref.dtype)
    for hd in range(GQA_HEADS):
        sl = slice(hd * LANES, (hd + 1) * LANES)
        qh = gqa_slot(p[:, C_QG + hd * LANES:C_QG + (hd + 1) * LANES], G_QN_GQA, G_QN_GQA_SW, AUG_Q_GQA)
        qg_ref[:, sl] = qh.astype(qg_ref.dtype)
    four_ref[...] = p[:, C_FOUR:C_END]


def _inproj_call(x, mod, wp, tabs, *, mod_row, rope, kv_only, tm):
    bsz, seq, d = x.shape
    ncol = C_KV_END if kv_only else C_END
    kernel = functools.partial(_inproj_kernel, mod_row=mod_row, rope=rope, kv_only=kv_only)
    row = lambda width, dt: (pl.BlockSpec((None, tm, width), lambda b, i: (b, i, 0)),
                             jax.ShapeDtypeStruct((bsz, seq, width), dt))
    outs = [row(MLA_HEADS * LANES, BF16), row(MLA_HEADS * LANES, BF16),
            row(GQA_KV_HEADS * LANES, BF16), row(GQA_KV_HEADS * LANES, BF16)]
    if not kv_only:
        outs += [row(MLA_HEADS * LANES, BF16), row(GQA_HEADS * LANES, BF16), row(FOUR_WIDTH, F32)]
    tab = lambda t: pl.BlockSpec((t.shape[0], tm, LANES), lambda b, i: (0, i, 0))
    uq_cols = MLA_HEADS * LANES * (2 if rope else 1)
    return pl.pallas_call(
        kernel,
        grid=(bsz, seq // tm),
        in_specs=[_const_spec(mod.shape),
                  pl.BlockSpec((None, tm, d), lambda b, i: (b, i, 0)),
                  _const_spec((1, d)),
                  pl.BlockSpec((d, ncol), lambda b, i: (0, 0), pipeline_mode=pl.Buffered(1)),
                  _const_spec((1, MLA_KV_RANK)), _const_spec((1, MLA_Q_RANK)),
                  _const_spec(wp["w_ukv"].shape),
                  pl.BlockSpec((MLA_Q_RANK, uq_cols), lambda b, i: (0, 0), pipeline_mode=pl.Buffered(1)),
                  _const_spec(wp["gslot"].shape), tab(tabs[0]), tab(tabs[1])],
        out_specs=[o[0] for o in outs],
        out_shape=[o[1] for o in outs],
        compiler_params=_cparams(2),
        name="inproj_kv" if kv_only else "inproj",
    )(mod, x, wp["g_norm1"], wp["w1"], wp["g_ckv"], wp["g_cq"], wp["w_ukv"], wp["w_uq"], wp["gslot"],
      tabs[0], tabs[1])


def _attn_kernel(q_ref, *refs, n_src, shared_kv, bounded):
    kv_refs, o_ref = refs[:2 * n_src], refs[2 * n_src]
    nt = (((1,), (1,)), ((), ()))
    tq = q_ref.shape[0]
    low_half = lax.broadcasted_iota(jnp.int32, (tq, LANES), 1) < SUM_LANE
    for pair in range(HEADS_PER_STEP // 2):
        halves = []
        for hh in range(2):
            i = pair * 2 + hh
            q = q_ref[:, i * LANES:(i + 1) * LANES]
            col = 0 if shared_kv else i * LANES
            ss = [lax.dot_general(q, kv_refs[2 * s][:, col:col + LANES], nt, preferred_element_type=F32)
                  for s in range(n_src)]
            if not bounded:
                m = functools.reduce(jnp.maximum, [jnp.max(s, axis=-1, keepdims=True) for s in ss])
                ss = [s - m for s in ss]
            o = functools.reduce(jnp.add, [_dot(jnp.exp2(ss[s]).astype(BF16), kv_refs[2 * s + 1][:, col:col + LANES])
                                           for s in range(n_src)])
            halves.append(o / o[:, SUM_LANE:SUM_LANE + 1])
        out = jnp.where(low_half, halves[0], pltpu.roll(halves[1], SUM_LANE, axis=1))
        o_ref[:, pair * LANES:(pair + 1) * LANES] = out.astype(o_ref.dtype)


def _attn_call(q, srcs, safe, *, shared_kv, tq, name):
    bsz, lq, qw = q.shape
    qcols = HEADS_PER_STEP * LANES
    n_groups = qw // qcols
    kw = LANES if shared_kv else qcols
    in_specs = [pl.BlockSpec((None, tq, qcols), lambda b, g, i: (b, i, g))]
    args = [q]
    for k, v in srcs:
        lk = k.shape[1]
        in_specs += [pl.BlockSpec((None, lk, kw), lambda b, g, i: (b, 0, g)),
                     pl.BlockSpec((None, lk, kw), lambda b, g, i: (b, 0, g))]
        args += [k, v]
    ow = qcols // 2

    def call(bounded):
        return pl.pallas_call(
            functools.partial(_attn_kernel, n_src=len(srcs), shared_kv=shared_kv, bounded=bounded),
            grid=(bsz, n_groups, lq // tq),
            in_specs=in_specs,
            out_specs=pl.BlockSpec((None, tq, ow), lambda b, g, i: (b, i, g)),
            out_shape=jax.ShapeDtypeStruct((bsz, lq, n_groups * ow), BF16),
            compiler_params=_cparams(3),
            name=name + ("" if bounded else "_rowmax"),
        )

    return lax.cond(safe, lambda *a: call(True)(*a), lambda *a: call(False)(*a), *args)


def _dft_grid_kernel(ka_ref, mr_ref, mi_ref, cs_ref, tc_ref, ts_ref, x_ref, o_ref, y_ref):
    tile = SUBLANES
    n_a, n_b, width = x_ref.shape
    n_r2 = y_ref.shape[1]
    n_r = n_r2 // 2
    reps = width // LANES
    for j in range(n_b // tile):
        xb = x_ref[:, j * tile:(j + 1) * tile, :].reshape(n_a * tile, width).astype(BF16)
        y = _dot(ka_ref[...], xb)
        y_ref[j * tile:(j + 1) * tile] = y.reshape(tile, n_r2, width)
    for j in range(n_r // tile):
        yr = y_ref[:, j * tile:(j + 1) * tile, :].reshape(n_b * tile, width)
        yi = y_ref[:, n_r + j * tile:n_r + (j + 1) * tile, :].reshape(n_b * tile, width)
        tc = jnp.concatenate([tc_ref[j]] * reps, axis=1)
        ts = jnp.concatenate([ts_ref[j]] * reps, axis=1)
        zr = (yr * tc - yi * ts).astype(BF16)
        zi = (yr * ts + yi * tc).astype(BF16)
        half = mr_ref.shape[0] // 2
        o_re = _dot(mr_ref[:half], zr) + _dot(mi_ref[:half], zi)
        o_im = _dot(mr_ref[half:], zr) + _dot(mi_ref[half:], zi)
        ocat = jnp.concatenate([o_re, o_im], axis=1).astype(BF16)
        res = _dot(ocat, cs_ref[...])
        o_ref[:, j * tile:(j + 1) * tile, :] = res.reshape(half // tile, tile, width)


def _dft_small_kernel(f_ref, cs_ref, x_ref, o_ref):
    x2 = _dot(f_ref[...], x_ref[...].astype(BF16))
    half = x2.shape[0] // 2
    ocat = jnp.concatenate([x2[:half], x2[half:]], axis=1).astype(BF16)
    o_ref[...] = _dot(ocat, cs_ref[...])


def _bf16_const(a):
    return jnp.asarray(np.asarray(a, np.float32), F32).astype(BF16)


def _dft_cos_sin(n):
    idx = np.arange(n)
    ang = 2.0 * np.pi * ((idx[:, None] * idx[None, :]) % n) / n
    return np.cos(ang), np.sin(ang)


def _channel_dft_const(seq, groups):
    c, s = _dft_cos_sin(FOUR_GROUP_DIM)
    eye = np.eye(groups)
    norm = 1.0 / np.sqrt(seq * FOUR_GROUP_DIM)
    return _bf16_const(np.concatenate([np.kron(eye, c), np.kron(eye, s)], axis=0) * norm)


def _fourier_call(f):
    bsz, seq, w = f.shape
    if seq < GRID_W * SUBLANES:
        c, s = _dft_cos_sin(seq)
        fmat = _bf16_const(np.concatenate([c, -s], axis=0))
        cs = _channel_dft_const(seq, FOUR_GROUPS)
        return pl.pallas_call(
            _dft_small_kernel,
            grid=(bsz,),
            in_specs=[_const_spec(fmat.shape), _const_spec(cs.shape),
                      pl.BlockSpec((None, seq, w), lambda b: (b, 0, 0))],
            out_specs=pl.BlockSpec((None, seq, w), lambda b: (b, 0, 0)),
            out_shape=jax.ShapeDtypeStruct((bsz, seq, w), F32),
            compiler_params=_cparams(1),
            name="dft_small",
        )(fmat, cs, f)

    rows = seq // GRID_W
    n_split = 2
    wh = w // n_split
    tile = SUBLANES
    eye = np.eye(tile)
    ca, sa = _dft_cos_sin(rows)
    fa = np.concatenate([ca, -sa], axis=0)
    ka = _bf16_const(np.einsum("ra,ij->iraj", fa, eye).reshape(tile * 2 * rows, rows * tile))
    cb, sb = _dft_cos_sin(GRID_W)
    expand = lambda m: np.einsum("pb,ij->pibj", m, eye).reshape(GRID_W * tile, GRID_W * tile)
    mr = _bf16_const(np.concatenate([expand(cb), expand(-sb)], axis=0))
    mi = _bf16_const(np.concatenate([expand(sb), expand(cb)], axis=0))
    cs = _channel_dft_const(seq, FOUR_GROUPS // n_split)
    ang = 2.0 * np.pi * (np.arange(GRID_W)[:, None] * np.arange(rows)[None, :]) / seq
    tw = lambda t: jnp.asarray(np.repeat(
        t.reshape(GRID_W, rows // tile, tile).transpose(1, 0, 2).reshape(rows // tile, GRID_W * tile)[:, :, None],
        LANES, axis=2), F32)
    tc, ts = tw(np.cos(ang)), tw(-np.sin(ang))
    o = pl.pallas_call(
        _dft_grid_kernel,
        grid=(bsz, n_split),
        in_specs=[_const_spec(ka.shape), _const_spec(mr.shape), _const_spec(mi.shape), _const_spec(cs.shape),
                  _const_spec(tc.shape), _const_spec(ts.shape),
                  pl.BlockSpec((None, rows, GRID_W, wh), lambda b, j: (b, 0, 0, j))],
        out_specs=pl.BlockSpec((None, GRID_W, rows, wh), lambda b, j: (b, 0, 0, j)),
        out_shape=jax.ShapeDtypeStruct((bsz, GRID_W, rows, w), F32),
        scratch_shapes=[pltpu.VMEM((GRID_W, 2 * rows, wh), F32)],
        compiler_params=_cparams(2),
        name="dft_grid",
    )(ka, mr, mi, cs, tc, ts, f.reshape(bsz, rows, GRID_W, w))
    return o.reshape(bsz, seq, w)


def _merge_kernel(mod_ref, x_ref, g1_ref, om_ref, og_ref, of_ref, wg_ref, wbm_ref, wbg_ref, wbf_ref, wo_ref,
                  o_ref, *, mod_row):
    d = x_ref.shape[-1]
    b = pl.program_id(0) if mod_row is None else mod_row
    shift = mod_ref[pl.ds(b, 1), 0:d]
    scale = mod_ref[pl.ds(b, 1), d:2 * d]
    gate1 = mod_ref[pl.ds(b, 1), 2 * d:3 * d]
    x = x_ref[...]
    h = _modulate(x, g1_ref[...], shift, scale).astype(BF16)
    y = None
    branches = ((om_ref[...], wbm_ref), (og_ref[...], wbg_ref), (of_ref[...].astype(BF16), wbf_ref))
    for n, (o, w_ref) in enumerate(branches):
        gate = 1.0 / (1.0 + jnp.exp(-_dot(h, wg_ref[:, n * d:(n + 1) * d])))
        t = gate * _dot(o, w_ref[...])
        y = t if y is None else y + t
    o_ref[...] = x + gate1 * _dot(y.astype(BF16), wo_ref[...])


def _merge_call(x, mod, o_mla, o_gqa, o_four, wp, *, mod_row, tm):
    bsz, seq, d = x.shape
    row = lambda width: pl.BlockSpec((None, tm, width), lambda b, i: (b, i, 0))
    return pl.pallas_call(
        functools.partial(_merge_kernel, mod_row=mod_row),
        grid=(bsz, seq // tm),
        in_specs=[_const_spec(mod.shape), row(d), _const_spec((1, d)),
                  row(o_mla.shape[-1]), row(o_gqa.shape[-1]), row(o_four.shape[-1]),
                  _const_spec(wp["w_gate"].shape), _const_spec(wp["w_br_mla"].shape),
                  _const_spec(wp["w_br_gqa"].shape), _const_spec(wp["w_four"].shape),
                  _const_spec(wp["w_o"].shape)],
        out_specs=row(d),
        out_shape=jax.ShapeDtypeStruct(x.shape, F32),
        compiler_params=_cparams(2),
        name="merge",
    )(mod, x, wp["g_norm1"], o_mla, o_gqa, o_four, wp["w_gate"], wp["w_br_mla"], wp["w_br_gqa"],
      wp["w_four"], wp["w_o"])


def _ffn_kernel(mod_ref, xp_ref, x_ref, xn_ref, g2_ref, wup_ref, cw_ref, cb_ref, wdn_ref, o_ref, *, mod_row):
    d = x_ref.shape[-1]
    tm = x_ref.shape[0]
    halo = xp_ref.shape[0]
    b = pl.program_id(0) if mod_row is None else mod_row
    i = pl.program_id(1)
    shift = mod_ref[pl.ds(b, 1), 3 * d:4 * d]
    scale = mod_ref[pl.ds(b, 1), 4 * d:5 * d]
    gate2 = mod_ref[pl.ds(b, 1), 5 * d:6 * d]
    x = x_ref[...]
    xe = jnp.concatenate([xp_ref[...], x, xn_ref[...]], axis=0)
    he = _modulate(xe, g2_ref[...], shift, scale)
    rid = lax.broadcasted_iota(jnp.int32, (tm + 2 * halo, 1), 0)
    outside = ((rid < halo) & (i == 0)) | ((rid >= tm + halo) & (i == pl.num_programs(1) - 1))
    he = jnp.where(outside, 0.0, he).astype(BF16)
    u = _dot(he, wup_ref[...])
    n = tm + 2 * halo
    up = pltpu.roll(u, 1, axis=0)[halo:halo + tm]
    dn = pltpu.roll(u, n - 1, axis=0)[halo:halo + tm]
    u = cw_ref[0:1, :] * up + cw_ref[1:2, :] * u[halo:halo + tm] + cw_ref[2:3, :] * dn + cb_ref[...]
    ff = u.shape[-1] // 2
    a, v = u[:, :ff], u[:, ff:]
    act = (a * (1.0 / (1.0 + jnp.exp(-a))) * v).astype(BF16)
    o_ref[...] = x + gate2 * _dot(act, wdn_ref[...])


def _ffn_call(x, mod, wp, *, mod_row, tm):
    bsz, seq, d = x.shape
    halo = SUBLANES
    nblk = seq // halo
    per = tm // halo
    return pl.pallas_call(
        functools.partial(_ffn_kernel, mod_row=mod_row),
        grid=(bsz, seq // tm),
        in_specs=[_const_spec(mod.shape),
                  pl.BlockSpec((None, halo, d), lambda b, i: (b, jnp.maximum(i * per - 1, 0), 0)),
                  pl.BlockSpec((None, tm, d), lambda b, i: (b, i, 0)),
                  pl.BlockSpec((None, halo, d), lambda b, i: (b, jnp.minimum((i + 1) * per, nblk - 1), 0)),
                  _const_spec((1, d)), _const_spec(wp["w_up"].shape), _const_spec(wp["conv_w"].shape),
                  _const_spec(wp["conv_b"].shape), _const_spec(wp["w_down"].shape)],
        out_specs=pl.BlockSpec((None, tm, d), lambda b, i: (b, i, 0)),
        out_shape=jax.ShapeDtypeStruct(x.shape, F32),
        compiler_params=_cparams(2),
        name="ffn",
    )(mod, x, x, x, wp["g_norm2"], wp["w_up"], wp["conv_w"], wp["conv_b"], wp["w_down"])


def _slots(w, n_slots, width, offset=0):
    k = w.shape[0]
    w3 = w.reshape(k, n_slots, width)
    w3 = jnp.pad(w3, ((0, 0), (0, 0), (offset, LANES - offset - width)))
    return w3.reshape(k, n_slots * LANES)


def _prep_layer(l, p):
    d = p["w_in"].shape[1]
    w_in = p["w_in"][l]
    o = 0
    c_kv = w_in[:, o:o + MLA_KV_RANK]; o += MLA_KV_RANK
    k_rope = w_in[:, o:o + MLA_ROPE]; o += MLA_ROPE
    k_g = w_in[:, o:o + GQA_KV_HEADS * GQA_DIM]; o += GQA_KV_HEADS * GQA_DIM
    v_g = w_in[:, o:o + GQA_KV_HEADS * GQA_DIM]; o += GQA_KV_HEADS * GQA_DIM
    c_q = w_in[:, o:o + MLA_Q_RANK]; o += MLA_Q_RANK
    q_g = w_in[:, o:o + GQA_HEADS * GQA_DIM]; o += GQA_HEADS * GQA_DIM
    four = w_in[:, o:o + FOUR_WIDTH]; o += FOUR_WIDTH
    gates = w_in[:, o:o + N_BRANCH * d]
    dup = lambda w, n: _slots(w, n, GQA_DIM, 0) + _slots(w, n, GQA_DIM, GQA_DIM)
    w1 = jnp.concatenate([
        c_kv,
        _slots(k_rope, 1, MLA_ROPE, MLA_NOPE),
        dup(k_g, GQA_KV_HEADS),
        _slots(v_g, GQA_KV_HEADS, GQA_DIM),
        c_q,
        dup(q_g, GQA_HEADS),
        four], axis=1).astype(BF16)

    w_ukv = p["w_ukv"][l].reshape(MLA_KV_RANK, MLA_HEADS, MLA_NOPE + MLA_V)
    k_nope = w_ukv[:, :, :MLA_NOPE].reshape(MLA_KV_RANK, MLA_HEADS * MLA_NOPE)
    v_mla = w_ukv[:, :, MLA_NOPE:].reshape(MLA_KV_RANK, MLA_HEADS * MLA_V)
    w_ukv2 = jnp.concatenate([_slots(k_nope, MLA_HEADS, MLA_NOPE), _slots(v_mla, MLA_HEADS, MLA_V)],
                             axis=1).astype(BF16)
    half = MLA_ROPE // 2
    w_uq = p["w_uq"][l].reshape(MLA_Q_RANK, MLA_HEADS, MLA_QK)
    partner = jnp.concatenate([-w_uq[:, :, MLA_NOPE + half:], w_uq[:, :, MLA_NOPE:MLA_NOPE + half]], axis=2)
    w_uq2 = jnp.concatenate([
        _slots(p["w_uq"][l], MLA_HEADS, MLA_QK),
        _slots(partner.reshape(MLA_Q_RANK, MLA_HEADS * MLA_ROPE), MLA_HEADS, MLA_ROPE, MLA_NOPE)],
        axis=1).astype(BF16)

    pad = lambda g: jnp.pad(g, (0, LANES - g.shape[0]))
    onehot = lambda lane: jnp.zeros((LANES,), F32).at[lane].set(1.0)
    bound_m = (MLA_QK ** 0.5) * jnp.max(jnp.abs(p["g_qn_mla"][l])) * jnp.max(jnp.abs(p["g_kn_mla"][l])) * (1 + 2.0 ** -6)
    bound_g = (GQA_DIM ** 0.5) * jnp.max(jnp.abs(p["g_qn_gqa"][l])) * jnp.max(jnp.abs(p["g_kn_gqa"][l])) * (1 + 2.0 ** -6)
    safe_m, safe_g = bound_m <= SAFE_SCORE_BOUND, bound_g <= SAFE_SCORE_BOUND
    swap = lambda g, lo, n: jnp.concatenate([g[:lo], g[lo + n:lo + 2 * n], g[lo:lo + n], g[lo + 2 * n:]])
    c_m, c_g = MLA_QK ** -0.5 * LOG2E, GQA_DIM ** -0.5 * LOG2E
    rope_only = (jnp.arange(LANES) >= MLA_NOPE).astype(F32)
    rows = [None] * 12
    rows[G_KN_MLA] = pad(p["g_kn_mla"][l])
    rows[G_QN_MLA] = pad(p["g_qn_mla"][l]) * c_m
    rows[G_QN_MLA_SW] = swap(pad(p["g_qn_mla"][l]), MLA_NOPE, half) * rope_only * c_m
    rows[G_KN_GQA] = pad(p["g_kn_gqa"][l])
    rows[G_KN_GQA_SW] = swap(pad(p["g_kn_gqa"][l]), 0, GQA_DIM // 2)
    rows[G_QN_GQA] = pad(p["g_qn_gqa"][l]) * c_g
    rows[G_QN_GQA_SW] = swap(pad(p["g_qn_gqa"][l]), 0, GQA_DIM // 2) * c_g
    rows[AUG_Q_MLA] = onehot(MLA_QK)
    rows[AUG_K_MLA] = onehot(MLA_QK) * jnp.where(safe_m, -bound_m * LOG2E, 0.0)
    rows[AUG_Q_GQA] = onehot(GQA_DIM)
    rows[AUG_K_GQA] = onehot(GQA_DIM) * jnp.where(safe_g, -bound_g * LOG2E, 0.0)
    rows[AUG_V] = onehot(SUM_LANE)
    gslot = jnp.stack(rows)
    gslot = jnp.pad(gslot, ((0, GSLOT_ROWS - gslot.shape[0]), (0, 0)))
    return {
        "w1": w1, "w_ukv": w_ukv2, "w_uq": w_uq2, "gslot": gslot, "safe_mla": safe_m, "safe_gqa": safe_g,
        "g_norm1": p["g_norm1"][l][None], "g_norm2": p["g_norm2"][l][None],
        "g_ckv": p["g_ckv"][l][None], "g_cq": p["g_cq"][l][None],
        "w_gate": gates.astype(BF16),
        "w_br_mla": p["w_br_mla"][l].astype(BF16), "w_br_gqa": p["w_br_gqa"][l].astype(BF16),
        "w_four": p["w_four"][l].astype(BF16), "w_o": p["w_o"][l].astype(BF16),
        "w_up": p["w_up"][l].astype(BF16), "conv_w": p["conv_w"][l], "conv_b": p["conv_b"][l][None],
        "w_down": p["w_down"][l].astype(BF16),
    }


def _rope_angles(seq, rot_dim):
    rows = seq // GRID_W
    row = np.repeat(np.arange(rows, dtype=np.float32), GRID_W)
    col = np.tile(np.arange(GRID_W, dtype=np.float32), rows)
    n_f = rot_dim // 4
    inv = (ROPE_BASE ** (-np.arange(n_f, dtype=np.float32) / n_f)).astype(np.float32)
    ang = jnp.asarray(np.concatenate([row[:, None] * inv, col[:, None] * inv], axis=-1), F32)
    return jnp.cos(ang), jnp.sin(ang)


def _rope_tables_mla(seq):
    cos, sin = _rope_angles(seq, MLA_ROPE)
    half = MLA_ROPE // 2
    z = lambda n: jnp.zeros((seq, n), F32)
    o = lambda n: jnp.ones((seq, n), F32)
    rest = LANES - MLA_QK
    return jnp.stack([
        jnp.concatenate([o(MLA_NOPE), cos, cos, o(rest)], axis=1),
        jnp.concatenate([z(MLA_NOPE), -sin, z(half), z(rest)], axis=1),
        jnp.concatenate([z(MLA_NOPE), z(half), sin, z(rest)], axis=1)])


def _rope_tables_gqa(seq):
    cos, sin = _rope_angles(seq, GQA_DIM)
    z = jnp.zeros((seq, LANES - GQA_DIM), F32)
    return jnp.stack([jnp.concatenate([cos, cos, z], axis=1), jnp.concatenate([-sin, sin, z], axis=1)])


def kernel(x, c, ctx, c_ctx, w_mod, b_mod, g_norm1, g_norm2, w_in, g_cq, g_ckv, w_uq, w_ukv, g_qn_mla, g_kn_mla,
           g_qn_gqa, g_kn_gqa, w_br_mla, w_br_gqa, w_four, w_o, w_up, conv_w, conv_b, w_down):
    params = dict(w_in=w_in, g_norm1=g_norm1, g_norm2=g_norm2, g_cq=g_cq, g_ckv=g_ckv, w_uq=w_uq, w_ukv=w_ukv,
                  g_qn_mla=g_qn_mla, g_kn_mla=g_kn_mla, g_qn_gqa=g_qn_gqa, g_kn_gqa=g_kn_gqa,
                  w_br_mla=w_br_mla, w_br_gqa=w_br_gqa, w_four=w_four, w_o=w_o, w_up=w_up, conv_w=conv_w,
                  conv_b=conv_b, w_down=w_down)
    bsz, seq, d = x.shape
    ctx_len = ctx.shape[1]
    depth = w_mod.shape[0]
    tm = 512
    tm_c = min(tm, ctx_len)
    tq = 512

    c_all = jnp.concatenate([c, c_ctx[None], jnp.zeros((SUBLANES - (bsz + 1) % SUBLANES, d), F32)], axis=0)
    mod_all = _mod_call(c_all, w_mod, b_mod)
    tabs = (_rope_tables_mla(seq), _rope_tables_gqa(seq))
    tabs_c = tuple(t[:, :ctx_len] for t in tabs)

    xc = ctx
    for l in range(depth):
        wp = _prep_layer(l, params)
        mod = mod_all[l]
        update_ctx = l < depth - 1
        km, vm, kg, vg, qm, qg, four = _inproj_call(x, mod, wp, tabs, mod_row=None, rope=True, kv_only=False, tm=tm)
        c_out = _inproj_call(xc, mod, wp, tabs_c, mod_row=bsz, rope=False, kv_only=not update_ctx, tm=tm_c)
        kmc, vmc, kgc, vgc = c_out[:4]
        o_mla = _attn_call(qm, [(kmc, vmc), (km, vm)], wp["safe_mla"], shared_kv=False, tq=tq, name="attn_mla")
        o_gqa = _attn_call(qg, [(kgc, vgc), (kg, vg)], wp["safe_gqa"], shared_kv=True, tq=tq, name="attn_gqa")
        o_four = _fourier_call(four)
        x = _merge_call(x, mod, o_mla, o_gqa, o_four, wp, mod_row=None, tm=tm)
        x = _ffn_call(x, mod, wp, mod_row=None, tm=tm)
        if update_ctx:
            qmc, qgc, fourc = c_out[4:]
            oc_mla = _attn_call(qmc, [(kmc, vmc)], wp["safe_mla"], shared_kv=False, tq=tm_c, name="attn_mla_ctx")
            oc_gqa = _attn_call(qgc, [(kgc, vgc)], wp["safe_gqa"], shared_kv=True, tq=tm_c, name="attn_gqa_ctx")
            oc_four = _fourier_call(fourc)
            xc = _merge_call(xc, mod, oc_mla, oc_gqa, oc_four, wp, mod_row=bsz, tm=tm_c)
            xc = _ffn_call(xc, mod, wp, mod_row=bsz, tm=tm_c)
    return x
```

```python
import functools

import numpy as np
import jax
import jax.numpy as jnp
from jax import lax
from jax.experimental import pallas as pl
from jax.experimental.pallas import tpu as pltpu

F32 = jnp.float32
BF16 = jnp.bfloat16

LANES = 128
SUBLANES = 8
VMEM_LIMIT = 56 * 1024 * 1024

GRID_W = 64
ROPE_BASE = 10000.0
EPS = 1e-6
MLA_HEADS = 8
MLA_NOPE = 64
MLA_ROPE = 32
MLA_QK = MLA_NOPE + MLA_ROPE
MLA_V = 64
MLA_Q_RANK = 384
MLA_KV_RANK = 256
GQA_HEADS = 8
GQA_KV_HEADS = 2
GQA_DIM = 64
FOUR_GROUPS = 4
FOUR_GROUP_DIM = 128
FOUR_WIDTH = FOUR_GROUPS * FOUR_GROUP_DIM
N_BRANCH = 3
SUM_LANE = 64
LOG2E = 1.4426950408889634
SAFE_SCORE_BOUND = 40.0
HEADS_PER_STEP = GQA_HEADS // GQA_KV_HEADS

C_CKV = 0
C_KROPE = C_CKV + MLA_KV_RANK
C_KG = C_KROPE + LANES
C_VG = C_KG + GQA_KV_HEADS * LANES
C_KV_END = C_VG + GQA_KV_HEADS * LANES
C_CQ = C_KV_END
C_QG = C_CQ + MLA_Q_RANK
C_FOUR = C_QG + GQA_HEADS * LANES
C_END = C_FOUR + FOUR_WIDTH


def _cparams(n_grid, vmem=VMEM_LIMIT):
    return pltpu.CompilerParams(dimension_semantics=("arbitrary",) * n_grid, vmem_limit_bytes=vmem)


def _const_spec(shape):
    nd = len(shape)
    return pl.BlockSpec(shape, lambda *_: (0,) * nd, pipeline_mode=pl.Buffered(1))


def _rms(x, g, dim):
    ss = jnp.sum(x * x, axis=-1, keepdims=True)
    return x * lax.rsqrt(ss * (1.0 / dim) + EPS) * g


def _modulate(x, g, shift, scale):
    return _rms(x, g, x.shape[-1]) * (1.0 + scale) + shift


def _rope_slot(x, tab_ref, half):
    up = pltpu.roll(x, LANES - half, axis=1)
    dn = pltpu.roll(x, half, axis=1)
    return x * tab_ref[0] + up * tab_ref[1] + dn * tab_ref[2]


def _dot(a, b):
    return jnp.dot(a, b, preferred_element_type=F32)


def _mod_kernel(c_ref, w_ref, b_ref, o_ref):
    c = c_ref[...]
    s = (c * (1.0 / (1.0 + jnp.exp(-c)))).astype(BF16)
    o_ref[...] = _dot(s, w_ref[...].astype(BF16)) + b_ref[...]


def _mod_call(c_all, w_mod, b_mod):
    depth, d, n = w_mod.shape
    tn = n // 4
    return pl.pallas_call(
        _mod_kernel,
        grid=(depth, n // tn),
        in_specs=[pl.BlockSpec(c_all.shape, lambda l, j: (0, 0)),
                  pl.BlockSpec((None, d, tn), lambda l, j: (l, 0, j)),
                  pl.BlockSpec((None, 1, tn), lambda l, j: (l, 0, j))],
        out_specs=pl.BlockSpec((None, c_all.shape[0], tn), lambda l, j: (l, 0, j)),
        out_shape=jax.ShapeDtypeStruct((depth, c_all.shape[0], n), F32),
        compiler_params=_cparams(2),
        name="modulation",
    )(c_all, w_mod, b_mod.reshape(depth, 1, n))


(G_KN_MLA, G_QN_MLA, G_QN_MLA_SW, G_KN_GQA, G_KN_GQA_SW, G_QN_GQA, G_QN_GQA_SW,
 AUG_Q_MLA, AUG_K_MLA, AUG_Q_GQA, AUG_K_GQA, AUG_V) = range(12)
GSLOT_ROWS = 2 * SUBLANES


def _row(ref, r):
    return ref[r:r + 1, :]


def _inv_rms(ss, dim):
    return lax.rsqrt(ss * (1.0 / dim) + EPS)


def _sumsq(x):
    return jnp.sum(x * x, axis=-1, keepdims=True)


def _inproj_kernel(mod_ref, x_ref, g1_ref, w1_ref, gckv_ref, gcq_ref, wukv_ref, wuq_ref, gslot_ref,
                   rm_ref, rg_ref, *out_refs, mod_row, rope, kv_only):
    d = x_ref.shape[-1]
    b = pl.program_id(0) if mod_row is None else mod_row
    shift = mod_ref[pl.ds(b, 1), 0:d]
    scale = mod_ref[pl.ds(b, 1), d:2 * d]
    h = _modulate(x_ref[...], g1_ref[...], shift, scale).astype(BF16)
    p = _dot(h, w1_ref[...])
    km_ref, vm_ref, kg_ref, vg_ref = out_refs[:4]
    aug_v = _row(gslot_ref, AUG_V)

    def gqa_slot(x, g_row, gsw_row, aug_row):
        r = _inv_rms(_sumsq(x), 2 * GQA_DIM)
        if rope:
            y = x * (_row(gslot_ref, g_row) * rg_ref[0]) \
                + pltpu.roll(x, GQA_DIM // 2, axis=1) * (_row(gslot_ref, gsw_row) * rg_ref[1])
        else:
            y = x * _row(gslot_ref, g_row)
        return y * r + _row(gslot_ref, aug_row)

    ckv = _rms(p[:, C_CKV:C_CKV + MLA_KV_RANK], gckv_ref[...], MLA_KV_RANK).astype(BF16)
    kv = _dot(ckv, wukv_ref[...])
    krope = p[:, C_KROPE:C_KROPE + LANES]
    gk = _row(gslot_ref, G_KN_MLA)
    kr = krope * gk
    if rope:
        kr = _rope_slot(kr, rm_ref, MLA_ROPE // 2)
    ss_r = _sumsq(krope)
    for hd in range(MLA_HEADS):
        sl = slice(hd * LANES, (hd + 1) * LANES)
        kh = kv[:, sl]
        r = _inv_rms(_sumsq(kh) + ss_r, MLA_QK)
        km_ref[:, sl] = ((kh * gk + kr) * r + _row(gslot_ref, AUG_K_MLA)).astype(km_ref.dtype)
        vm_ref[hd] = (kv[:, MLA_HEADS * LANES + hd * LANES:MLA_HEADS * LANES + (hd + 1) * LANES]
                      + aug_v).T.astype(vm_ref.dtype)

    for hd in range(GQA_KV_HEADS):
        sl = slice(hd * LANES, (hd + 1) * LANES)
        kh = gqa_slot(p[:, C_KG + hd * LANES:C_KG + (hd + 1) * LANES], G_KN_GQA, G_KN_GQA_SW, AUG_K_GQA)
        kg_ref[:, sl] = kh.astype(kg_ref.dtype)
        vg_ref[hd] = (p[:, C_VG + hd * LANES:C_VG + (hd + 1) * LANES] + aug_v).T.astype(vg_ref.dtype)
    if kv_only:
        return

    qm_ref, qg_ref, four_ref = out_refs[4:]
    cq = _rms(p[:, C_CQ:C_CQ + MLA_Q_RANK], gcq_ref[...], MLA_Q_RANK).astype(BF16)
    qm = _dot(cq, wuq_ref[...])
    a_cos = _row(gslot_ref, G_QN_MLA)
    if rope:
        a_cos = a_cos * rm_ref[0]
        a_sin = _row(gslot_ref, G_QN_MLA_SW) * (rm_ref[2] - rm_ref[1])
    for hd in range(MLA_HEADS):
        sl = slice(hd * LANES, (hd + 1) * LANES)
        q = qm[:, sl]
        y = q * a_cos
        if rope:
            y = y + qm[:, MLA_HEADS * LANES + hd * LANES:MLA_HEADS * LANES + (hd + 1) * LANES] * a_sin
        r = _inv_rms(_sumsq(q), MLA_QK)
        qm_ref[:, sl] = (y * r + _row(gslot_ref, AUG_Q_MLA)).astype(qm_ref.dtype)
    for hd in range(GQA_HEADS):
        sl = slice(hd * LANES, (hd + 1) * LANES)
        qh = gqa_slot(p[:, C_QG + hd * LANES:C_QG + (hd + 1) * LANES], G_QN_GQA, G_QN_GQA_SW, AUG_Q_GQA)
        qg_ref[:, sl] = qh.astype(qg_ref.dtype)
    four_ref[...] = p[:, C_FOUR:C_END]


def _inproj_call(x, mod, wp, tabs, *, mod_row, rope, kv_only, tm):
    bsz, seq, d = x.shape
    ncol = C_KV_END if kv_only else C_END
    kernel = functools.partial(_inproj_kernel, mod_row=mod_row, rope=rope, kv_only=kv_only)
    row = lambda width, dt: (pl.BlockSpec((None, tm, width), lambda b, i: (b, i, 0)),
                             jax.ShapeDtypeStruct((bsz, seq, width), dt))
    col = lambda slots: (pl.BlockSpec((None, slots, LANES, tm), lambda b, i: (b, 0, 0, i)),
                         jax.ShapeDtypeStruct((bsz, slots, LANES, seq), BF16))
    outs = [row(MLA_HEADS * LANES, BF16), col(MLA_HEADS), row(GQA_KV_HEADS * LANES, BF16), col(GQA_KV_HEADS)]
    if not kv_only:
        outs += [row(MLA_HEADS * LANES, BF16), row(GQA_HEADS * LANES, BF16), row(FOUR_WIDTH, F32)]
    tab = lambda t: pl.BlockSpec((t.shape[0], tm, LANES), lambda b, i: (0, i, 0))
    uq_cols = MLA_HEADS * LANES * (2 if rope else 1)
    return pl.pallas_call(
        kernel,
        grid=(bsz, seq // tm),
        in_specs=[_const_spec(mod.shape),
                  pl.BlockSpec((None, tm, d), lambda b, i: (b, i, 0)),
                  _const_spec((1, d)),
                  pl.BlockSpec((d, ncol), lambda b, i: (0, 0), pipeline_mode=pl.Buffered(1)),
                  _const_spec((1, MLA_KV_RANK)), _const_spec((1, MLA_Q_RANK)),
                  _const_spec(wp["w_ukv"].shape),
                  pl.BlockSpec((MLA_Q_RANK, uq_cols), lambda b, i: (0, 0), pipeline_mode=pl.Buffered(1)),
                  _const_spec(wp["gslot"].shape), tab(tabs[0]), tab(tabs[1])],
        out_specs=[o[0] for o in outs],
        out_shape=[o[1] for o in outs],
        compiler_params=_cparams(2),
        name="inproj_kv" if kv_only else "inproj",
    )(mod, x, wp["g_norm1"], wp["w1"], wp["g_ckv"], wp["g_cq"], wp["w_ukv"], wp["w_uq"], wp["gslot"],
      tabs[0], tabs[1])


def _attn_kernel(q_ref, *refs, n_src, shared_kv, bounded):
    kv_refs, o_ref = refs[:2 * n_src], refs[2 * n_src]
    tq = q_ref.shape[0]
    low_half = lax.broadcasted_iota(jnp.int32, (tq, LANES), 1) < SUM_LANE
    for pair in range(HEADS_PER_STEP // 2):
        halves = []
        for hh in range(2):
            i = pair * 2 + hh
            col = 0 if shared_kv else i * LANES
            vi = 0 if shared_kv else i
            qt = q_ref[:, i * LANES:(i + 1) * LANES].T
            ss = [_dot(kv_refs[2 * s][:, col:col + LANES], qt) for s in range(n_src)]
            if not bounded:
                m = functools.reduce(jnp.maximum, [jnp.max(s, axis=0, keepdims=True) for s in ss])
                ss = [s - m for s in ss]
            ot = functools.reduce(jnp.add, [_dot(kv_refs[2 * s + 1][vi], jnp.exp2(ss[s]).astype(BF16))
                                            for s in range(n_src)])
            halves.append((ot / ot[SUM_LANE:SUM_LANE + 1, :]).T)
        out = jnp.where(low_half, halves[0], pltpu.roll(halves[1], SUM_LANE, axis=1))
        o_ref[:, pair * LANES:(pair + 1) * LANES] = out.astype(o_ref.dtype)


def _attn_call(q, srcs, safe, *, shared_kv, tq, name):
    bsz, lq, qw = q.shape
    qcols = HEADS_PER_STEP * LANES
    n_groups = qw // qcols
    kw = LANES if shared_kv else qcols
    nv = 1 if shared_kv else HEADS_PER_STEP
    in_specs = [pl.BlockSpec((None, tq, qcols), lambda b, g, i: (b, i, g))]
    args = [q]
    for k, v in srcs:
        lk = k.shape[1]
        in_specs += [pl.BlockSpec((None, lk, kw), lambda b, g, i: (b, 0, g)),
                     pl.BlockSpec((None, nv, LANES, lk), lambda b, g, i: (b, g, 0, 0))]
        args += [k, v]
    ow = qcols // 2

    def call(bounded):
        return pl.pallas_call(
            functools.partial(_attn_kernel, n_src=len(srcs), shared_kv=shared_kv, bounded=bounded),
            grid=(bsz, n_groups, lq // tq),
            in_specs=in_specs,
            out_specs=pl.BlockSpec((None, tq, ow), lambda b, g, i: (b, i, g)),
            out_shape=jax.ShapeDtypeStruct((bsz, lq, n_groups * ow), BF16),
            compiler_params=_cparams(3),
            name=name + ("" if bounded else "_rowmax"),
        )

    return lax.cond(safe, lambda *a: call(True)(*a), lambda *a: call(False)(*a), *args)


def _dft_grid_kernel(ka_ref, mr_ref, mi_ref, cs_ref, tc_ref, ts_ref, x_ref, o_ref, y_ref):
    tile = SUBLANES
    n_a, n_b, width = x_ref.shape
    n_r2 = y_ref.shape[1]
    n_r = n_r2 // 2
    reps = width // LANES
    for j in range(n_b // tile):
        xb = x_ref[:, j * tile:(j + 1) * tile, :].reshape(n_a * tile, width).astype(BF16)
        y = _dot(ka_ref[...], xb)
        y_ref[j * tile:(j + 1) * tile] = y.reshape(tile, n_r2, width)
    for j in range(n_r // tile):
        yr = y_ref[:, j * tile:(j + 1) * tile, :].reshape(n_b * tile, width)
        yi = y_ref[:, n_r + j * tile:n_r + (j + 1) * tile, :].reshape(n_b * tile, width)
        tc = jnp.concatenate([tc_ref[j]] * reps, axis=1)
        ts = jnp.concatenate([ts_ref[j]] * reps, axis=1)
        zr = (yr * tc - yi * ts).astype(BF16)
        zi = (yr * ts + yi * tc).astype(BF16)
        half = mr_ref.shape[0] // 2
        o_re = _dot(mr_ref[:half], zr) + _dot(mi_ref[:half], zi)
        o_im = _dot(mr_ref[half:], zr) + _dot(mi_ref[half:], zi)
        ocat = jnp.concatenate([o_re, o_im], axis=1).astype(BF16)
        res = _dot(ocat, cs_ref[...])
        o_ref[:, j * tile:(j + 1) * tile, :] = res.reshape(half // tile, tile, width)


def _dft_small_kernel(f_ref, cs_ref, x_ref, o_ref):
    x2 = _dot(f_ref[...], x_ref[...].astype(BF16))
    half = x2.shape[0] // 2
    ocat = jnp.concatenate([x2[:half], x2[half:]], axis=1).astype(BF16)
    o_ref[...] = _dot(ocat, cs_ref[...])


def _bf16_const(a):
    return jnp.asarray(np.asarray(a, np.float32), F32).astype(BF16)


def _dft_cos_sin(n):
    idx = np.arange(n)
    ang = 2.0 * np.pi * ((idx[:, None] * idx[None, :]) % n) / n
    return np.cos(ang), np.sin(ang)


def _channel_dft_const(seq, groups):
    c, s = _dft_cos_sin(FOUR_GROUP_DIM)
    eye = np.eye(groups)
    norm = 1.0 / np.sqrt(seq * FOUR_GROUP_DIM)
    return _bf16_const(np.concatenate([np.kron(eye, c), np.kron(eye, s)], axis=0) * norm)


def _fourier_call(f):
    bsz, seq, w = f.shape
    if seq < GRID_W * SUBLANES:
        c, s = _dft_cos_sin(seq)
        fmat = _bf16_const(np.concatenate([c, -s], axis=0))
        cs = _channel_dft_const(seq, FOUR_GROUPS)
        return pl.pallas_call(
            _dft_small_kernel,
            grid=(bsz,),
            in_specs=[_const_spec(fmat.shape), _const_spec(cs.shape),
                      pl.BlockSpec((None, seq, w), lambda b: (b, 0, 0))],
            out_specs=pl.BlockSpec((None, seq, w), lambda b: (b, 0, 0)),
            out_shape=jax.ShapeDtypeStruct((bsz, seq, w), F32),
            compiler_params=_cparams(1),
            name="dft_small",
        )(fmat, cs, f)

    rows = seq // GRID_W
    n_split = 2
    wh = w // n_split
    tile = SUBLANES
    eye = np.eye(tile)
    ca, sa = _dft_cos_sin(rows)
    fa = np.concatenate([ca, -sa], axis=0)
    ka = _bf16_const(np.einsum("ra,ij->iraj", fa, eye).reshape(tile * 2 * rows, rows * tile))
    cb, sb = _dft_cos_sin(GRID_W)
    expand = lambda m: np.einsum("pb,ij->pibj", m, eye).reshape(GRID_W * tile, GRID_W * tile)
    mr = _bf16_const(np.concatenate([expand(cb), expand(-sb)], axis=0))
    mi = _bf16_const(np.concatenate([expand(sb), expand(cb)], axis=0))
    cs = _channel_dft_const(seq, FOUR_GROUPS // n_split)
    ang = 2.0 * np.pi * (np.arange(GRID_W)[:, None] * np.arange(rows)[None, :]) / seq
    tw = lambda t: jnp.asarray(np.repeat(
        t.reshape(GRID_W, rows // tile, tile).transpose(1, 0, 2).reshape(rows // tile, GRID_W * tile)[:, :, None],
        LANES, axis=2), F32)
    tc, ts = tw(np.cos(ang)), tw(-np.sin(ang))
    o = pl.pallas_call(
        _dft_grid_kernel,
        grid=(bsz, n_split),
        in_specs=[_const_spec(ka.shape), _const_spec(mr.shape), _const_spec(mi.shape), _const_spec(cs.shape),
                  _const_spec(tc.shape), _const_spec(ts.shape),
                  pl.BlockSpec((None, rows, GRID_W, wh), lambda b, j: (b, 0, 0, j))],
        out_specs=pl.BlockSpec((None, GRID_W, rows, wh), lambda b, j: (b, 0, 0, j)),
        out_shape=jax.ShapeDtypeStruct((bsz, GRID_W, rows, w), F32),
        scratch_shapes=[pltpu.VMEM((GRID_W, 2 * rows, wh), F32)],
        compiler_params=_cparams(2),
        name="dft_grid",
    )(ka, mr, mi, cs, tc, ts, f.reshape(bsz, rows, GRID_W, w))
    return o.reshape(bsz, seq, w)


def _merge_kernel(mod_ref, x_ref, g1_ref, om_ref, og_ref, of_ref, wg_ref, wbm_ref, wbg_ref, wbf_ref, wo_ref,
                  o_ref, *, mod_row):
    d = x_ref.shape[-1]
    b = pl.program_id(0) if mod_row is None else mod_row
    shift = mod_ref[pl.ds(b, 1), 0:d]
    scale = mod_ref[pl.ds(b, 1), d:2 * d]
    gate1 = mod_ref[pl.ds(b, 1), 2 * d:3 * d]
    x = x_ref[...]
    h = _modulate(x, g1_ref[...], shift, scale).astype(BF16)
    y = None
    branches = ((om_ref[...], wbm_ref), (og_ref[...], wbg_ref), (of_ref[...].astype(BF16), wbf_ref))
    for n, (o, w_ref) in enumerate(branches):
        gate = 1.0 / (1.0 + jnp.exp(-_dot(h, wg_ref[:, n * d:(n + 1) * d])))
        t = gate * _dot(o, w_ref[...])
        y = t if y is None else y + t
    o_ref[...] = x + gate1 * _dot(y.astype(BF16), wo_ref[...])


def _merge_call(x, mod, o_mla, o_gqa, o_four, wp, *, mod_row, tm):
    bsz, seq, d = x.shape
    row = lambda width: pl.BlockSpec((None, tm, width), lambda b, i: (b, i, 0))
    return pl.pallas_call(
        functools.partial(_merge_kernel, mod_row=mod_row),
        grid=(bsz, seq // tm),
        in_specs=[_const_spec(mod.shape), row(d), _const_spec((1, d)),
                  row(o_mla.shape[-1]), row(o_gqa.shape[-1]), row(o_four.shape[-1]),
                  _const_spec(wp["w_gate"].shape), _const_spec(wp["w_br_mla"].shape),
                  _const_spec(wp["w_br_gqa"].shape), _const_spec(wp["w_four"].shape),
                  _const_spec(wp["w_o"].shape)],
        out_specs=row(d),
        out_shape=jax.ShapeDtypeStruct(x.shape, F32),
        compiler_params=_cparams(2),
        name="merge",
    )(mod, x, wp["g_norm1"], o_mla, o_gqa, o_four, wp["w_gate"], wp["w_br_mla"], wp["w_br_gqa"],
      wp["w_four"], wp["w_o"])


def _ffn_kernel(mod_ref, xp_ref, x_ref, xn_ref, g2_ref, wup_ref, cw_ref, cb_ref, wdn_ref, o_ref, *, mod_row):
    d = x_ref.shape[-1]
    tm = x_ref.shape[0]
    halo = xp_ref.shape[0]
    b = pl.program_id(0) if mod_row is None else mod_row
    i = pl.program_id(1)
    shift = mod_ref[pl.ds(b, 1), 3 * d:4 * d]
    scale = mod_ref[pl.ds(b, 1), 4 * d:5 * d]
    gate2 = mod_ref[pl.ds(b, 1), 5 * d:6 * d]
    x = x_ref[...]
    xe = jnp.concatenate([xp_ref[...], x, xn_ref[...]], axis=0)
    he = _modulate(xe, g2_ref[...], shift, scale)
    rid = lax.broadcasted_iota(jnp.int32, (tm + 2 * halo, 1), 0)
    outside = ((rid < halo) & (i == 0)) | ((rid >= tm + halo) & (i == pl.num_programs(1) - 1))
    he = jnp.where(outside, 0.0, he).astype(BF16)
    u = _dot(he, wup_ref[...])
    n = tm + 2 * halo
    up = pltpu.roll(u, 1, axis=0)[halo:halo + tm]
    dn = pltpu.roll(u, n - 1, axis=0)[halo:halo + tm]
    u = cw_ref[0:1, :] * up + cw_ref[1:2, :] * u[halo:halo + tm] + cw_ref[2:3, :] * dn + cb_ref[...]
    ff = u.shape[-1] // 2
    a, v = u[:, :ff], u[:, ff:]
    act = (a * (1.0 / (1.0 + jnp.exp(-a))) * v).astype(BF16)
    o_ref[...] = x + gate2 * _dot(act, wdn_ref[...])


def _ffn_call(x, mod, wp, *, mod_row, tm):
    bsz, seq, d = x.shape
    halo = SUBLANES
    nblk = seq // halo
    per = tm // halo
    return pl.pallas_call(
        functools.partial(_ffn_kernel, mod_row=mod_row),
        grid=(bsz, seq // tm),
        in_specs=[_const_spec(mod.shape),
                  pl.BlockSpec((None, halo, d), lambda b, i: (b, jnp.maximum(i * per - 1, 0), 0)),
                  pl.BlockSpec((None, tm, d), lambda b, i: (b, i, 0)),
                  pl.BlockSpec((None, halo, d), lambda b, i: (b, jnp.minimum((i + 1) * per, nblk - 1), 0)),
                  _const_spec((1, d)), _const_spec(wp["w_up"].shape), _const_spec(wp["conv_w"].shape),
                  _const_spec(wp["conv_b"].shape), _const_spec(wp["w_down"].shape)],
        out_specs=pl.BlockSpec((None, tm, d), lambda b, i: (b, i, 0)),
        out_shape=jax.ShapeDtypeStruct(x.shape, F32),
        compiler_params=_cparams(2),
        name="ffn",
    )(mod, x, x, x, wp["g_norm2"], wp["w_up"], wp["conv_w"], wp["conv_b"], wp["w_down"])


def _slots(w, n_slots, width, offset=0):
    k = w.shape[0]
    w3 = w.reshape(k, n_slots, width)
    w3 = jnp.pad(w3, ((0, 0), (0, 0), (offset, LANES - offset - width)))
    return w3.reshape(k, n_slots * LANES)


def _prep_layer(l, p):
    d = p["w_in"].shape[1]
    w_in = p["w_in"][l]
    o = 0
    c_kv = w_in[:, o:o + MLA_KV_RANK]; o += MLA_KV_RANK
    k_rope = w_in[:, o:o + MLA_ROPE]; o += MLA_ROPE
    k_g = w_in[:, o:o + GQA_KV_HEADS * GQA_DIM]; o += GQA_KV_HEADS * GQA_DIM
    v_g = w_in[:, o:o + GQA_KV_HEADS * GQA_DIM]; o += GQA_KV_HEADS * GQA_DIM
    c_q = w_in[:, o:o + MLA_Q_RANK]; o += MLA_Q_RANK
    q_g = w_in[:, o:o + GQA_HEADS * GQA_DIM]; o += GQA_HEADS * GQA_DIM
    four = w_in[:, o:o + FOUR_WIDTH]; o += FOUR_WIDTH
    gates = w_in[:, o:o + N_BRANCH * d]
    dup = lambda w, n: _slots(w, n, GQA_DIM, 0) + _slots(w, n, GQA_DIM, GQA_DIM)
    w1 = jnp.concatenate([
        c_kv,
        _slots(k_rope, 1, MLA_ROPE, MLA_NOPE),
        dup(k_g, GQA_KV_HEADS),
        _slots(v_g, GQA_KV_HEADS, GQA_DIM),
        c_q,
        dup(q_g, GQA_HEADS),
        four], axis=1).astype(BF16)

    w_ukv = p["w_ukv"][l].reshape(MLA_KV_RANK, MLA_HEADS, MLA_NOPE + MLA_V)
    k_nope = w_ukv[:, :, :MLA_NOPE].reshape(MLA_KV_RANK, MLA_HEADS * MLA_NOPE)
    v_mla = w_ukv[:, :, MLA_NOPE:].reshape(MLA_KV_RANK, MLA_HEADS * MLA_V)
    w_ukv2 = jnp.concatenate([_slots(k_nope, MLA_HEADS, MLA_NOPE), _slots(v_mla, MLA_HEADS, MLA_V)],
                             axis=1).astype(BF16)
    half = MLA_ROPE // 2
    w_uq = p["w_uq"][l].reshape(MLA_Q_RANK, MLA_HEADS, MLA_QK)
    partner = jnp.concatenate([-w_uq[:, :, MLA_NOPE + half:], w_uq[:, :, MLA_NOPE:MLA_NOPE + half]], axis=2)
    w_uq2 = jnp.concatenate([
        _slots(p["w_uq"][l], MLA_HEADS, MLA_QK),
        _slots(partner.reshape(MLA_Q_RANK, MLA_HEADS * MLA_ROPE), MLA_HEADS, MLA_ROPE, MLA_NOPE)],
        axis=1).astype(BF16)

    pad = lambda g: jnp.pad(g, (0, LANES - g.shape[0]))
    onehot = lambda lane: jnp.zeros((LANES,), F32).at[lane].set(1.0)
    bound_m = (MLA_QK ** 0.5) * jnp.max(jnp.abs(p["g_qn_mla"][l])) * jnp.max(jnp.abs(p["g_kn_mla"][l])) * (1 + 2.0 ** -6)
    bound_g = (GQA_DIM ** 0.5) * jnp.max(jnp.abs(p["g_qn_gqa"][l])) * jnp.max(jnp.abs(p["g_kn_gqa"][l])) * (1 + 2.0 ** -6)
    safe_m, safe_g = bound_m <= SAFE_SCORE_BOUND, bound_g <= SAFE_SCORE_BOUND
    swap = lambda g, lo, n: jnp.concatenate([g[:lo], g[lo + n:lo + 2 * n], g[lo:lo + n], g[lo + 2 * n:]])
    c_m, c_g = MLA_QK ** -0.5 * LOG2E, GQA_DIM ** -0.5 * LOG2E
    rope_only = (jnp.arange(LANES) >= MLA_NOPE).astype(F32)
    rows = [None] * 12
    rows[G_KN_MLA] = pad(p["g_kn_mla"][l])
    rows[G_QN_MLA] = pad(p["g_qn_mla"][l]) * c_m
    rows[G_QN_MLA_SW] = swap(pad(p["g_qn_mla"][l]), MLA_NOPE, half) * rope_only * c_m
    rows[G_KN_GQA] = pad(p["g_kn_gqa"][l])
    rows[G_KN_GQA_SW] = swap(pad(p["g_kn_gqa"][l]), 0, GQA_DIM // 2)
    rows[G_QN_GQA] = pad(p["g_qn_gqa"][l]) * c_g
    rows[G_QN_GQA_SW] = swap(pad(p["g_qn_gqa"][l]), 0, GQA_DIM // 2) * c_g
    rows[AUG_Q_MLA] = onehot(MLA_QK)
    rows[AUG_K_MLA] = onehot(MLA_QK) * jnp.where(safe_m, -bound_m * LOG2E, 0.0)
    rows[AUG_Q_GQA] = onehot(GQA_DIM)
    rows[AUG_K_GQA] = onehot(GQA_DIM) * jnp.where(safe_g, -bound_g * LOG2E, 0.0)
    rows[AUG_V] = onehot(SUM_LANE)
    gslot = jnp.stack(rows)
    gslot = jnp.pad(gslot, ((0, GSLOT_ROWS - gslot.shape[0]), (0, 0)))
    return {
        "w1": w1, "w_ukv": w_ukv2, "w_uq": w_uq2, "gslot": gslot, "safe_mla": safe_m, "safe_gqa": safe_g,
        "g_norm1": p["g_norm1"][l][None], "g_norm2": p["g_norm2"][l][None],
        "g_ckv": p["g_ckv"][l][None], "g_cq": p["g_cq"][l][None],
        "w_gate": gates.astype(BF16),
        "w_br_mla": p["w_br_mla"][l].astype(BF16), "w_br_gqa": p["w_br_gqa"][l].astype(BF16),
        "w_four": p["w_four"][l].astype(BF16), "w_o": p["w_o"][l].astype(BF16),
        "w_up": p["w_up"][l].astype(BF16), "conv_w": p["conv_w"][l], "conv_b": p["conv_b"][l][None],
        "w_down": p["w_down"][l].astype(BF16),
    }


def _rope_angles(seq, rot_dim):
    rows = seq // GRID_W
    row = np.repeat(np.arange(rows, dtype=np.float32), GRID_W)
    col = np.tile(np.arange(GRID_W, dtype=np.float32), rows)
    n_f = rot_dim // 4
    inv = (ROPE_BASE ** (-np.arange(n_f, dtype=np.float32) / n_f)).astype(np.float32)
    ang = jnp.asarray(np.concatenate([row[:, None] * inv, col[:, None] * inv], axis=-1), F32)
    return jnp.cos(ang), jnp.sin(ang)


def _rope_tables_mla(seq):
    cos, sin = _rope_angles(seq, MLA_ROPE)
    half = MLA_ROPE // 2
    z = lambda n: jnp.zeros((seq, n), F32)
    o = lambda n: jnp.ones((seq, n), F32)
    rest = LANES - MLA_QK
    return jnp.stack([
        jnp.concatenate([o(MLA_NOPE), cos, cos, o(rest)], axis=1),
        jnp.concatenate([z(MLA_NOPE), -sin, z(half), z(rest)], axis=1),
        jnp.concatenate([z(MLA_NOPE), z(half), sin, z(rest)], axis=1)])


def _rope_tables_gqa(seq):
    cos, sin = _rope_angles(seq, GQA_DIM)
    z = jnp.zeros((seq, LANES - GQA_DIM), F32)
    return jnp.stack([jnp.concatenate([cos, cos, z], axis=1), jnp.concatenate([-sin, sin, z], axis=1)])


def kernel(x, c, ctx, c_ctx, w_mod, b_mod, g_norm1, g_norm2, w_in, g_cq, g_ckv, w_uq, w_ukv, g_qn_mla, g_kn_mla,
           g_qn_gqa, g_kn_gqa, w_br_mla, w_br_gqa, w_four, w_o, w_up, conv_w, conv_b, w_down):
    params = dict(w_in=w_in, g_norm1=g_norm1, g_norm2=g_norm2, g_cq=g_cq, g_ckv=g_ckv, w_uq=w_uq, w_ukv=w_ukv,
                  g_qn_mla=g_qn_mla, g_kn_mla=g_kn_mla, g_qn_gqa=g_qn_gqa, g_kn_gqa=g_kn_gqa,
                  w_br_mla=w_br_mla, w_br_gqa=w_br_gqa, w_four=w_four, w_o=w_o, w_up=w_up, conv_w=conv_w,
                  conv_b=conv_b, w_down=w_down)
    bsz, seq, d = x.shape
    ctx_len = ctx.shape[1]
    depth = w_mod.shape[0]
    tm = 512
    tm_c = min(tm, ctx_len)
    tq = 512

    c_all = jnp.concatenate([c, c_ctx[None], jnp.zeros((SUBLANES - (bsz + 1) % SUBLANES, d), F32)], axis=0)
    mod_all = _mod_call(c_all, w_mod, b_mod)
    tabs = (_rope_tables_mla(seq), _rope_tables_gqa(seq))
    tabs_c = tuple(t[:, :ctx_len] for t in tabs)

    xc = ctx
    for l in range(depth):
        wp = _prep_layer(l, params)
        mod = mod_all[l]
        update_ctx = l < depth - 1
        km, vm, kg, vg, qm, qg, four = _inproj_call(x, mod, wp, tabs, mod_row=None, rope=True, kv_only=False, tm=tm)
        c_out = _inproj_call(xc, mod, wp, tabs_c, mod_row=bsz, rope=False, kv_only=not update_ctx, tm=tm_c)
        kmc, vmc, kgc, vgc = c_out[:4]
        o_mla = _attn_call(qm, [(kmc, vmc), (km, vm)], wp["safe_mla"], shared_kv=False, tq=tq, name="attn_mla")
        o_gqa = _attn_call(qg, [(kgc, vgc), (kg, vg)], wp["safe_gqa"], shared_kv=True, tq=tq, name="attn_gqa")
        o_four = _fourier_call(four)
        x = _merge_call(x, mod, o_mla, o_gqa, o_four, wp, mod_row=None, tm=tm)
        x = _ffn_call(x, mod, wp, mod_row=None, tm=tm)
        if update_ctx:
            qmc, qgc, fourc = c_out[4:]
            oc_mla = _attn_call(qmc, [(kmc, vmc)], wp["safe_mla"], shared_kv=False, tq=tm_c, name="attn_mla_ctx")
            oc_gqa = _attn_call(qgc, [(kgc, vgc)], wp["safe_gqa"], shared_kv=True, tq=tm_c, name="attn_gqa_ctx")
            oc_four = _fourier_call(fourc)
            xc = _merge_call(xc, mod, oc_mla, oc_gqa, oc_four, wp, mod_row=bsz, tm=tm_c)
            xc = _ffn_call(xc, mod, wp, mod_row=bsz, tm=tm_c)
    return x
```

```python
import functools

import numpy as np
import jax
import jax.numpy as jnp
from jax import lax
from jax.experimental import pallas as pl
from jax.experimental.pallas import tpu as pltpu

F32 = jnp.float32
BF16 = jnp.bfloat16

LANES = 128
SUBLANES = 8
VMEM_LIMIT = 56 * 1024 * 1024

GRID_W = 64
ROPE_BASE = 10000.0
EPS = 1e-6
MLA_HEADS = 8
MLA_NOPE = 64
MLA_ROPE = 32
MLA_QK = MLA_NOPE + MLA_ROPE
MLA_V = 64
MLA_Q_RANK = 384
MLA_KV_RANK = 256
GQA_HEADS = 8
GQA_KV_HEADS = 2
GQA_DIM = 64
FOUR_GROUPS = 4
FOUR_GROUP_DIM = 128
FOUR_WIDTH = FOUR_GROUPS * FOUR_GROUP_DIM
N_BRANCH = 3
SUM_LANE = 64
LOG2E = 1.4426950408889634
SAFE_SCORE_BOUND = 40.0
HEADS_PER_STEP = GQA_HEADS // GQA_KV_HEADS

STACKED = ("w_br_mla", "w_br_gqa", "w_four", "w_o", "w_up", "w_down")

C_CKV = 0
C_KROPE = C_CKV + MLA_KV_RANK
C_KG = C_KROPE + LANES
C_VG = C_KG + GQA_KV_HEADS * LANES
C_KV_END = C_VG + GQA_KV_HEADS * LANES
C_CQ = C_KV_END
C_QG = C_CQ + MLA_Q_RANK
C_FOUR = C_QG + GQA_HEADS * LANES
C_END = C_FOUR + FOUR_WIDTH


def _cparams(n_grid, vmem=VMEM_LIMIT):
    return pltpu.CompilerParams(dimension_semantics=("arbitrary",) * n_grid, vmem_limit_bytes=vmem)


def _const_spec(shape):
    nd = len(shape)
    return pl.BlockSpec(shape, lambda *_: (0,) * nd, pipeline_mode=pl.Buffered(1))


def _layer_spec(stack, l):
    return pl.BlockSpec((None,) + stack.shape[1:], lambda *_: (l, 0, 0), pipeline_mode=pl.Buffered(1))


def _rms(x, g, dim):
    ss = jnp.sum(x * x, axis=-1, keepdims=True)
    return x * lax.rsqrt(ss * (1.0 / dim) + EPS) * g


def _modulate(x, g, shift, scale):
    return _rms(x, g, x.shape[-1]) * (1.0 + scale) + shift


def _rope_slot(x, tab_ref, half):
    up = pltpu.roll(x, LANES - half, axis=1)
    dn = pltpu.roll(x, half, axis=1)
    return x * tab_ref[0] + up * tab_ref[1] + dn * tab_ref[2]


def _dot(a, b):
    return jnp.dot(a, b, preferred_element_type=F32)


def _dot_nt(a, bt):
    return lax.dot_general(a, bt, (((1,), (1,)), ((), ())), preferred_element_type=F32)


def _mod_kernel(c_ref, w_ref, b_ref, o_ref):
    c = c_ref[...]
    s = (c * (1.0 / (1.0 + jnp.exp(-c)))).astype(BF16)
    o_ref[...] = _dot(s, w_ref[...].astype(BF16)) + b_ref[...]


def _mod_call(c_all, w_mod, b_mod):
    depth, d, n = w_mod.shape
    tn = n // 4
    return pl.pallas_call(
        _mod_kernel,
        grid=(depth, n // tn),
        in_specs=[pl.BlockSpec(c_all.shape, lambda l, j: (0, 0)),
                  pl.BlockSpec((None, d, tn), lambda l, j: (l, 0, j)),
                  pl.BlockSpec((None, 1, tn), lambda l, j: (l, 0, j))],
        out_specs=pl.BlockSpec((None, c_all.shape[0], tn), lambda l, j: (l, 0, j)),
        out_shape=jax.ShapeDtypeStruct((depth, c_all.shape[0], n), F32),
        compiler_params=_cparams(2),
        name="modulation",
    )(c_all, w_mod, b_mod.reshape(depth, 1, n))


(G_KN_MLA, G_QN_MLA, G_QN_MLA_SW, G_KN_GQA, G_KN_GQA_SW, G_QN_GQA, G_QN_GQA_SW,
 AUG_Q_MLA, AUG_K_MLA, AUG_Q_GQA, AUG_K_GQA, AUG_V) = range(12)
GSLOT_ROWS = 2 * SUBLANES


def _row(ref, r):
    return ref[r:r + 1, :]


def _inv_rms(ss, dim):
    return lax.rsqrt(ss * (1.0 / dim) + EPS)


def _sumsq(x):
    return jnp.sum(x * x, axis=-1, keepdims=True)


def _inproj_kernel(mod_ref, x_ref, g1_ref, w1_ref, gckv_ref, gcq_ref, wukv_ref, wuq_ref, gslot_ref,
                   rm_ref, rg_ref, *out_refs, mod_row, rope, kv_only):
    d = x_ref.shape[-1]
    b = pl.program_id(0) if mod_row is None else mod_row
    shift = mod_ref[pl.ds(b, 1), 0:d]
    scale = mod_ref[pl.ds(b, 1), d:2 * d]
    h = _modulate(x_ref[...], g1_ref[...], shift, scale).astype(BF16)
    p = _dot_nt(h, w1_ref[...])
    km_ref, vm_ref, kg_ref, vg_ref = out_refs[:4]
    aug_v = _row(gslot_ref, AUG_V)

    def gqa_slot(x, g_row, gsw_row, aug_row):
        r = _inv_rms(_sumsq(x), 2 * GQA_DIM)
        if rope:
            y = x * (_row(gslot_ref, g_row) * rg_ref[0]) \
                + pltpu.roll(x, GQA_DIM // 2, axis=1) * (_row(gslot_ref, gsw_row) * rg_ref[1])
        else:
            y = x * _row(gslot_ref, g_row)
        return y * r + _row(gslot_ref, aug_row)

    ckv = _rms(p[:, C_CKV:C_CKV + MLA_KV_RANK], gckv_ref[...], MLA_KV_RANK).astype(BF16)
    kv = _dot(ckv, wukv_ref[...])
    krope = p[:, C_KROPE:C_KROPE + LANES]
    gk = _row(gslot_ref, G_KN_MLA)
    kr = krope * gk
    if rope:
        kr = _rope_slot(kr, rm_ref, MLA_ROPE // 2)
    ss_r = _sumsq(krope)
    for hd in range(MLA_HEADS):
        sl = slice(hd * LANES, (hd + 1) * LANES)
        kh = kv[:, sl]
        r = _inv_rms(_sumsq(kh) + ss_r, MLA_QK)
        km_ref[:, sl] = ((kh * gk + kr) * r + _row(gslot_ref, AUG_K_MLA)).astype(km_ref.dtype)
        vm_ref[hd] = (kv[:, MLA_HEADS * LANES + hd * LANES:MLA_HEADS * LANES + (hd + 1) * LANES]
                      + aug_v).T.astype(vm_ref.dtype)

    for hd in range(GQA_KV_HEADS):
        sl = slice(hd * LANES, (hd + 1) * LANES)
        kh = gqa_slot(p[:, C_KG + hd * LANES:C_KG + (hd + 1) * LANES], G_KN_GQA, G_KN_GQA_SW, AUG_K_GQA)
        kg_ref[:, sl] = kh.astype(kg_ref.dtype)
        vg_ref[hd] = (p[:, C_VG + hd * LANES:C_VG + (hd + 1) * LANES] + aug_v).T.astype(vg_ref.dtype)
    if kv_only:
        return

    qm_ref, qg_ref, four_ref = out_refs[4:]
    cq = _rms(p[:, C_CQ:C_CQ + MLA_Q_RANK], gcq_ref[...], MLA_Q_RANK).astype(BF16)
    qm_ref[...] = _dot(cq, wuq_ref[...]).astype(qm_ref.dtype)
    qg_ref[...] = p[:, C_QG:C_FOUR].astype(qg_ref.dtype)
    four_ref[...] = p[:, C_FOUR:C_END]


def _inproj_call(x, mod, wp, tabs, *, mod_row, rope, kv_only, tm):
    bsz, seq, d = x.shape
    ncol = C_KV_END if kv_only else C_END
    kernel = functools.partial(_inproj_kernel, mod_row=mod_row, rope=rope, kv_only=kv_only)
    row = lambda width, dt: (pl.BlockSpec((None, tm, width), lambda b, i: (b, i, 0)),
                             jax.ShapeDtypeStruct((bsz, seq, width), dt))
    col = lambda slots: (pl.BlockSpec((None, slots, LANES, tm), lambda b, i: (b, 0, 0, i)),
                         jax.ShapeDtypeStruct((bsz, slots, LANES, seq), BF16))
    outs = [row(MLA_HEADS * LANES, BF16), col(MLA_HEADS), row(GQA_KV_HEADS * LANES, BF16), col(GQA_KV_HEADS)]
    if not kv_only:
        outs += [row(MLA_HEADS * LANES, BF16), row(GQA_HEADS * LANES, BF16), row(FOUR_WIDTH, F32)]
    tab = lambda t: pl.BlockSpec((t.shape[0], tm, LANES), lambda b, i: (0, i, 0))
    return pl.pallas_call(
        kernel,
        grid=(bsz, seq // tm),
        in_specs=[_const_spec(mod.shape),
                  pl.BlockSpec((None, tm, d), lambda b, i: (b, i, 0)),
                  _const_spec((1, d)),
                  pl.BlockSpec((ncol, d), lambda b, i: (0, 0), pipeline_mode=pl.Buffered(1)),
                  _const_spec((1, MLA_KV_RANK)), _const_spec((1, MLA_Q_RANK)),
                  _const_spec(wp["w_ukv"].shape), _const_spec(wp["w_uq"].shape),
                  _const_spec(wp["gslot"].shape), tab(tabs[0]), tab(tabs[1])],
        out_specs=[o[0] for o in outs],
        out_shape=[o[1] for o in outs],
        compiler_params=_cparams(2),
        name="inproj_kv" if kv_only else "inproj",
    )(mod, x, wp["g_norm1"], wp["w1"], wp["g_ckv"], wp["g_cq"], wp["w_ukv"], wp["w_uq"], wp["gslot"],
      tabs[0], tabs[1])


def _attn_kernel(gslot_ref, tab_ref, q_ref, *refs, n_src, shared_kv, bounded, rope):
    kv_refs, o_ref = refs[:2 * n_src], refs[2 * n_src]
    tq = q_ref.shape[0]
    low_half = lax.broadcasted_iota(jnp.int32, (tq, LANES), 1) < SUM_LANE
    if shared_kv:
        g_row, gsw_row, aug_row, dim, shifts = G_QN_GQA, G_QN_GQA_SW, AUG_Q_GQA, 2 * GQA_DIM, (GQA_DIM // 2,)
    else:
        g_row, gsw_row, aug_row, dim = G_QN_MLA, G_QN_MLA_SW, AUG_Q_MLA, MLA_QK
        shifts = (LANES - MLA_ROPE // 2, MLA_ROPE // 2)
    gain, aug = _row(gslot_ref, g_row), _row(gslot_ref, aug_row)
    if rope:
        gain = gain * tab_ref[0]
        sins = [_row(gslot_ref, gsw_row) * tab_ref[1 + t] for t in range(len(shifts))]

    def query(x):
        y = x * gain
        if rope:
            for shift, sin in zip(shifts, sins):
                y = y + pltpu.roll(x, shift, axis=1) * sin
        return y * _inv_rms(_sumsq(x), dim) + aug

    for pair in range(HEADS_PER_STEP // 2):
        halves = []
        for hh in range(2):
            i = pair * 2 + hh
            col = 0 if shared_kv else i * LANES
            vi = 0 if shared_kv else i
            qt = query(q_ref[:, i * LANES:(i + 1) * LANES].astype(F32)).astype(BF16).T
            ss = [_dot(kv_refs[2 * s][:, col:col + LANES], qt) for s in range(n_src)]
            if not bounded:
                m = functools.reduce(jnp.maximum, [jnp.max(s, axis=0, keepdims=True) for s in ss])
                ss = [s - m for s in ss]
            ot = functools.reduce(jnp.add, [_dot(kv_refs[2 * s + 1][vi], jnp.exp2(ss[s]).astype(BF16))
                                            for s in range(n_src)])
            halves.append((ot / ot[SUM_LANE:SUM_LANE + 1, :]).T)
        out = jnp.where(low_half, halves[0], pltpu.roll(halves[1], SUM_LANE, axis=1))
        o_ref[:, pair * LANES:(pair + 1) * LANES] = out.astype(o_ref.dtype)


def _attn_call(q, srcs, safe, gslot, tab, *, shared_kv, tq, name):
    bsz, lq, qw = q.shape
    qcols = HEADS_PER_STEP * LANES
    n_groups = qw // qcols
    kw = LANES if shared_kv else qcols
    nv = 1 if shared_kv else HEADS_PER_STEP
    rope = tab is not None
    if not rope:
        tab = jnp.zeros((1, SUBLANES, LANES), F32)
    in_specs = [_const_spec(gslot.shape),
                pl.BlockSpec((tab.shape[0], tq if rope else SUBLANES, LANES), lambda b, g, i: (0, i if rope else 0, 0)),
                pl.BlockSpec((None, tq, qcols), lambda b, g, i: (b, i, g))]
    args = [gslot, tab, q]
    for k, v in srcs:
        lk = k.shape[1]
        in_specs += [pl.BlockSpec((None, lk, kw), lambda b, g, i: (b, 0, g)),
                     pl.BlockSpec((None, nv, LANES, lk), lambda b, g, i: (b, g, 0, 0))]
        args += [k, v]
    ow = qcols // 2

    def call(bounded):
        return pl.pallas_call(
            functools.partial(_attn_kernel, n_src=len(srcs), shared_kv=shared_kv, bounded=bounded, rope=rope),
            grid=(bsz, n_groups, lq // tq),
            in_specs=in_specs,
            out_specs=pl.BlockSpec((None, tq, ow), lambda b, g, i: (b, i, g)),
            out_shape=jax.ShapeDtypeStruct((bsz, lq, n_groups * ow), BF16),
            compiler_params=_cparams(3),
            name=name + ("" if bounded else "_rowmax"),
        )

    return lax.cond(safe, lambda *a: call(True)(*a), lambda *a: call(False)(*a), *args)


def _dft_grid_kernel(ka_ref, mr_ref, mi_ref, cs_ref, tc_ref, ts_ref, x_ref, o_ref, y_ref):
    tile = SUBLANES
    n_a, n_b, width = x_ref.shape
    n_r2 = y_ref.shape[1]
    n_r = n_r2 // 2
    reps = width // LANES
    for j in range(n_b // tile):
        xb = x_ref[:, j * tile:(j + 1) * tile, :].reshape(n_a * tile, width).astype(BF16)
        y = _dot(ka_ref[...], xb)
        y_ref[j * tile:(j + 1) * tile] = y.reshape(tile, n_r2, width)
    for j in range(n_r // tile):
        yr = y_ref[:, j * tile:(j + 1) * tile, :].reshape(n_b * tile, width)
        yi = y_ref[:, n_r + j * tile:n_r + (j + 1) * tile, :].reshape(n_b * tile, width)
        tc = jnp.concatenate([tc_ref[j]] * reps, axis=1)
        ts = jnp.concatenate([ts_ref[j]] * reps, axis=1)
        zr = (yr * tc - yi * ts).astype(BF16)
        zi = (yr * ts + yi * tc).astype(BF16)
        half = mr_ref.shape[0] // 2
        o_re = _dot(mr_ref[:half], zr) + _dot(mi_ref[:half], zi)
        o_im = _dot(mr_ref[half:], zr) + _dot(mi_ref[half:], zi)
        ocat = jnp.concatenate([o_re, o_im], axis=1).astype(BF16)
        res = _dot(ocat, cs_ref[...])
        o_ref[:, j * tile:(j + 1) * tile, :] = res.reshape(half // tile, tile, width)


def _dft_small_kernel(f_ref, cs_ref, x_ref, o_ref):
    x2 = _dot(f_ref[...], x_ref[...].astype(BF16))
    half = x2.shape[0] // 2
    ocat = jnp.concatenate([x2[:half], x2[half:]], axis=1).astype(BF16)
    o_ref[...] = _dot(ocat, cs_ref[...])


def _bf16_const(a):
    return jnp.asarray(np.asarray(a, np.float32), F32).astype(BF16)


def _dft_cos_sin(n):
    idx = np.arange(n)
    ang = 2.0 * np.pi * ((idx[:, None] * idx[None, :]) % n) / n
    return np.cos(ang), np.sin(ang)


def _channel_dft_const(seq, groups):
    c, s = _dft_cos_sin(FOUR_GROUP_DIM)
    eye = np.eye(groups)
    norm = 1.0 / np.sqrt(seq * FOUR_GROUP_DIM)
    return _bf16_const(np.concatenate([np.kron(eye, c), np.kron(eye, s)], axis=0) * norm)


def _fourier_call(f):
    bsz, seq, w = f.shape
    if seq < GRID_W * SUBLANES:
        c, s = _dft_cos_sin(seq)
        fmat = _bf16_const(np.concatenate([c, -s], axis=0))
        cs = _channel_dft_const(seq, FOUR_GROUPS)
        return pl.pallas_call(
            _dft_small_kernel,
            grid=(bsz,),
            in_specs=[_const_spec(fmat.shape), _const_spec(cs.shape),
                      pl.BlockSpec((None, seq, w), lambda b: (b, 0, 0))],
            out_specs=pl.BlockSpec((None, seq, w), lambda b: (b, 0, 0)),
            out_shape=jax.ShapeDtypeStruct((bsz, seq, w), F32),
            compiler_params=_cparams(1),
            name="dft_small",
        )(fmat, cs, f)

    rows = seq // GRID_W
    n_split = 2
    wh = w // n_split
    tile = SUBLANES
    eye = np.eye(tile)
    ca, sa = _dft_cos_sin(rows)
    fa = np.concatenate([ca, -sa], axis=0)
    ka = _bf16_const(np.einsum("ra,ij->iraj", fa, eye).reshape(tile * 2 * rows, rows * tile))
    cb, sb = _dft_cos_sin(GRID_W)
    expand = lambda m: np.einsum("pb,ij->pibj", m, eye).reshape(GRID_W * tile, GRID_W * tile)
    mr = _bf16_const(np.concatenate([expand(cb), expand(-sb)], axis=0))
    mi = _bf16_const(np.concatenate([expand(sb), expand(cb)], axis=0))
    cs = _channel_dft_const(seq, FOUR_GROUPS // n_split)
    ang = 2.0 * np.pi * (np.arange(GRID_W)[:, None] * np.arange(rows)[None, :]) / seq
    tw = lambda t: jnp.asarray(np.repeat(
        t.reshape(GRID_W, rows // tile, tile).transpose(1, 0, 2).reshape(rows // tile, GRID_W * tile)[:, :, None],
        LANES, axis=2), F32)
    tc, ts = tw(np.cos(ang)), tw(-np.sin(ang))
    o = pl.pallas_call(
        _dft_grid_kernel,
        grid=(bsz, n_split),
        in_specs=[_const_spec(ka.shape), _const_spec(mr.shape), _const_spec(mi.shape), _const_spec(cs.shape),
                  _const_spec(tc.shape), _const_spec(ts.shape),
                  pl.BlockSpec((None, rows, GRID_W, wh), lambda b, j: (b, 0, 0, j))],
        out_specs=pl.BlockSpec((None, GRID_W, rows, wh), lambda b, j: (b, 0, 0, j)),
        out_shape=jax.ShapeDtypeStruct((bsz, GRID_W, rows, w), F32),
        scratch_shapes=[pltpu.VMEM((GRID_W, 2 * rows, wh), F32)],
        compiler_params=_cparams(2),
        name="dft_grid",
    )(ka, mr, mi, cs, tc, ts, f.reshape(bsz, rows, GRID_W, w))
    return o.reshape(bsz, seq, w)


def _merge_kernel(mod_ref, x_ref, g1_ref, om_ref, og_ref, of_ref, wg_ref, wbm_ref, wbg_ref, wbf_ref, wo_ref,
                  o_ref, *, mod_row):
    d = x_ref.shape[-1]
    b = pl.program_id(0) if mod_row is None else mod_row
    shift = mod_ref[pl.ds(b, 1), 0:d]
    scale = mod_ref[pl.ds(b, 1), d:2 * d]
    gate1 = mod_ref[pl.ds(b, 1), 2 * d:3 * d]
    x = x_ref[...]
    h = _modulate(x, g1_ref[...], shift, scale).astype(BF16)
    y = None
    branches = ((om_ref[...], wbm_ref), (og_ref[...], wbg_ref), (of_ref[...].astype(BF16), wbf_ref))
    for n, (o, w_ref) in enumerate(branches):
        gate = 1.0 / (1.0 + jnp.exp(-_dot_nt(h, wg_ref[n * d:(n + 1) * d, :])))
        t = gate * _dot(o, w_ref[...])
        y = t if y is None else y + t
    o_ref[...] = x + gate1 * _dot(y.astype(BF16), wo_ref[...])


def _merge_call(x, mod, o_mla, o_gqa, o_four, wp, *, mod_row, tm):
    bsz, seq, d = x.shape
    row = lambda width: pl.BlockSpec((None, tm, width), lambda b, i: (b, i, 0))
    return pl.pallas_call(
        functools.partial(_merge_kernel, mod_row=mod_row),
        grid=(bsz, seq // tm),
        in_specs=[_const_spec(mod.shape), row(d), _const_spec((1, d)),
                  row(o_mla.shape[-1]), row(o_gqa.shape[-1]), row(o_four.shape[-1]),
                  _const_spec(wp["w_gate"].shape), _layer_spec(wp["w_br_mla"], wp["layer"]),
                  _layer_spec(wp["w_br_gqa"], wp["layer"]), _layer_spec(wp["w_four"], wp["layer"]),
                  _layer_spec(wp["w_o"], wp["layer"])],
        out_specs=row(d),
        out_shape=jax.ShapeDtypeStruct(x.shape, F32),
        compiler_params=_cparams(2),
        name="merge",
    )(mod, x, wp["g_norm1"], o_mla, o_gqa, o_four, wp["w_gate"], wp["w_br_mla"], wp["w_br_gqa"],
      wp["w_four"], wp["w_o"])


def _ffn_kernel(mod_ref, xp_ref, x_ref, xn_ref, g2_ref, wup_ref, cw_ref, cb_ref, wdn_ref, o_ref, *, mod_row):
    d = x_ref.shape[-1]
    tm = x_ref.shape[0]
    halo = xp_ref.shape[0]
    b = pl.program_id(0) if mod_row is None else mod_row
    i = pl.program_id(1)
    shift = mod_ref[pl.ds(b, 1), 3 * d:4 * d]
    scale = mod_ref[pl.ds(b, 1), 4 * d:5 * d]
    gate2 = mod_ref[pl.ds(b, 1), 5 * d:6 * d]
    x = x_ref[...]
    xe = jnp.concatenate([xp_ref[...], x, xn_ref[...]], axis=0)
    he = _modulate(xe, g2_ref[...], shift, scale)
    rid = lax.broadcasted_iota(jnp.int32, (tm + 2 * halo, 1), 0)
    outside = ((rid < halo) & (i == 0)) | ((rid >= tm + halo) & (i == pl.num_programs(1) - 1))
    he = jnp.where(outside, 0.0, he).astype(BF16)
    u = _dot(he, wup_ref[...])
    n = tm + 2 * halo
    up = pltpu.roll(u, 1, axis=0)[halo:halo + tm]
    dn = pltpu.roll(u, n - 1, axis=0)[halo:halo + tm]
    u = cw_ref[0:1, :] * up + cw_ref[1:2, :] * u[halo:halo + tm] + cw_ref[2:3, :] * dn + cb_ref[...]
    ff = u.shape[-1] // 2
    a, v = u[:, :ff], u[:, ff:]
    act = (a * (1.0 / (1.0 + jnp.exp(-a))) * v).astype(BF16)
    o_ref[...] = x + gate2 * _dot(act, wdn_ref[...])


def _ffn_call(x, mod, wp, *, mod_row, tm):
    bsz, seq, d = x.shape
    halo = SUBLANES
    nblk = seq // halo
    per = tm // halo
    return pl.pallas_call(
        functools.partial(_ffn_kernel, mod_row=mod_row),
        grid=(bsz, seq // tm),
        in_specs=[_const_spec(mod.shape),
                  pl.BlockSpec((None, halo, d), lambda b, i: (b, jnp.maximum(i * per - 1, 0), 0)),
                  pl.BlockSpec((None, tm, d), lambda b, i: (b, i, 0)),
                  pl.BlockSpec((None, halo, d), lambda b, i: (b, jnp.minimum((i + 1) * per, nblk - 1), 0)),
                  _const_spec((1, d)), _layer_spec(wp["w_up"], wp["layer"]), _const_spec(wp["conv_w"].shape),
                  _const_spec(wp["conv_b"].shape), _layer_spec(wp["w_down"], wp["layer"])],
        out_specs=pl.BlockSpec((None, tm, d), lambda b, i: (b, i, 0)),
        out_shape=jax.ShapeDtypeStruct(x.shape, F32),
        compiler_params=_cparams(2),
        name="ffn",
    )(mod, x, x, x, wp["g_norm2"], wp["w_up"], wp["conv_w"], wp["conv_b"], wp["w_down"])


def _slots(w, n_slots, width, offset=0):
    k = w.shape[0]
    w3 = w.reshape(k, n_slots, width)
    w3 = jnp.pad(w3, ((0, 0), (0, 0), (offset, LANES - offset - width)))
    return w3.reshape(k, n_slots * LANES)


def _slots_t(wt, n_slots, width, offset=0):
    k = wt.shape[1]
    w3 = wt.reshape(n_slots, width, k)
    w3 = jnp.pad(w3, ((0, 0), (offset, LANES - offset - width), (0, 0)))
    return w3.reshape(n_slots * LANES, k)


def _prep_layer(l, p):
    d = p["w_in"].shape[1]
    w_t = jnp.swapaxes(p["w_in"], 1, 2)[l].astype(BF16)
    o = 0
    c_kv = w_t[o:o + MLA_KV_RANK]; o += MLA_KV_RANK
    k_rope = w_t[o:o + MLA_ROPE]; o += MLA_ROPE
    k_g = w_t[o:o + GQA_KV_HEADS * GQA_DIM]; o += GQA_KV_HEADS * GQA_DIM
    v_g = w_t[o:o + GQA_KV_HEADS * GQA_DIM]; o += GQA_KV_HEADS * GQA_DIM
    c_q = w_t[o:o + MLA_Q_RANK]; o += MLA_Q_RANK
    q_g = w_t[o:o + GQA_HEADS * GQA_DIM]; o += GQA_HEADS * GQA_DIM
    four = w_t[o:o + FOUR_WIDTH]; o += FOUR_WIDTH
    gates = w_t[o:o + N_BRANCH * d]
    dup = lambda w, n: _slots_t(w, n, GQA_DIM, 0) + _slots_t(w, n, GQA_DIM, GQA_DIM)
    w1 = jnp.concatenate([
        c_kv,
        _slots_t(k_rope, 1, MLA_ROPE, MLA_NOPE),
        dup(k_g, GQA_KV_HEADS),
        _slots_t(v_g, GQA_KV_HEADS, GQA_DIM),
        c_q,
        dup(q_g, GQA_HEADS),
        four], axis=0)

    w_ukv = p["w_ukv"][l].reshape(MLA_KV_RANK, MLA_HEADS, MLA_NOPE + MLA_V)
    k_nope = w_ukv[:, :, :MLA_NOPE].reshape(MLA_KV_RANK, MLA_HEADS * MLA_NOPE)
    v_mla = w_ukv[:, :, MLA_NOPE:].reshape(MLA_KV_RANK, MLA_HEADS * MLA_V)
    w_ukv2 = jnp.concatenate([_slots(k_nope, MLA_HEADS, MLA_NOPE), _slots(v_mla, MLA_HEADS, MLA_V)],
                             axis=1).astype(BF16)
    w_uq2 = _slots(p["w_uq"][l], MLA_HEADS, MLA_QK).astype(BF16)

    pad = lambda g: jnp.pad(g, (0, LANES - g.shape[0]))
    onehot = lambda lane: jnp.zeros((LANES,), F32).at[lane].set(1.0)
    bound_m = (MLA_QK ** 0.5) * jnp.max(jnp.abs(p["g_qn_mla"][l])) * jnp.max(jnp.abs(p["g_kn_mla"][l])) * (1 + 2.0 ** -6)
    bound_g = (GQA_DIM ** 0.5) * jnp.max(jnp.abs(p["g_qn_gqa"][l])) * jnp.max(jnp.abs(p["g_kn_gqa"][l])) * (1 + 2.0 ** -6)
    safe_m, safe_g = bound_m <= SAFE_SCORE_BOUND, bound_g <= SAFE_SCORE_BOUND
    swap = lambda g, lo, n: jnp.concatenate([g[:lo], g[lo + n:lo + 2 * n], g[lo:lo + n], g[lo + 2 * n:]])
    c_m, c_g = MLA_QK ** -0.5 * LOG2E, GQA_DIM ** -0.5 * LOG2E
    rope_only = (jnp.arange(LANES) >= MLA_NOPE).astype(F32)
    rows = [None] * 12
    rows[G_KN_MLA] = pad(p["g_kn_mla"][l])
    rows[G_QN_MLA] = pad(p["g_qn_mla"][l]) * c_m
    rows[G_QN_MLA_SW] = swap(pad(p["g_qn_mla"][l]), MLA_NOPE, MLA_ROPE // 2) * rope_only * c_m
    rows[G_KN_GQA] = pad(p["g_kn_gqa"][l])
    rows[G_KN_GQA_SW] = swap(pad(p["g_kn_gqa"][l]), 0, GQA_DIM // 2)
    rows[G_QN_GQA] = pad(p["g_qn_gqa"][l]) * c_g
    rows[G_QN_GQA_SW] = swap(pad(p["g_qn_gqa"][l]), 0, GQA_DIM // 2) * c_g
    rows[AUG_Q_MLA] = onehot(MLA_QK)
    rows[AUG_K_MLA] = onehot(MLA_QK) * jnp.where(safe_m, -bound_m * LOG2E, 0.0)
    rows[AUG_Q_GQA] = onehot(GQA_DIM)
    rows[AUG_K_GQA] = onehot(GQA_DIM) * jnp.where(safe_g, -bound_g * LOG2E, 0.0)
    rows[AUG_V] = onehot(SUM_LANE)
    gslot = jnp.stack(rows)
    gslot = jnp.pad(gslot, ((0, GSLOT_ROWS - gslot.shape[0]), (0, 0)))
    return {
        "w1": w1, "w_ukv": w_ukv2, "w_uq": w_uq2, "gslot": gslot, "safe_mla": safe_m, "safe_gqa": safe_g,
        "g_norm1": p["g_norm1"][l][None], "g_norm2": p["g_norm2"][l][None],
        "g_ckv": p["g_ckv"][l][None], "g_cq": p["g_cq"][l][None],
        "w_gate": gates, "conv_w": p["conv_w"][l], "conv_b": p["conv_b"][l][None],
        "layer": l, **{name: p["bf16"][name] for name in STACKED},
    }


def _rope_angles(seq, rot_dim):
    rows = seq // GRID_W
    row = np.repeat(np.arange(rows, dtype=np.float32), GRID_W)
    col = np.tile(np.arange(GRID_W, dtype=np.float32), rows)
    n_f = rot_dim // 4
    inv = (ROPE_BASE ** (-np.arange(n_f, dtype=np.float32) / n_f)).astype(np.float32)
    ang = jnp.asarray(np.concatenate([row[:, None] * inv, col[:, None] * inv], axis=-1), F32)
    return jnp.cos(ang), jnp.sin(ang)


def _rope_tables_mla(seq):
    cos, sin = _rope_angles(seq, MLA_ROPE)
    half = MLA_ROPE // 2
    z = lambda n: jnp.zeros((seq, n), F32)
    o = lambda n: jnp.ones((seq, n), F32)
    rest = LANES - MLA_QK
    return jnp.stack([
        jnp.concatenate([o(MLA_NOPE), cos, cos, o(rest)], axis=1),
        jnp.concatenate([z(MLA_NOPE), -sin, z(half), z(rest)], axis=1),
        jnp.concatenate([z(MLA_NOPE), z(half), sin, z(rest)], axis=1)])


def _rope_tables_gqa(seq):
    cos, sin = _rope_angles(seq, GQA_DIM)
    z = jnp.zeros((seq, LANES - GQA_DIM), F32)
    return jnp.stack([jnp.concatenate([cos, cos, z], axis=1), jnp.concatenate([-sin, sin, z], axis=1)])


def kernel(x, c, ctx, c_ctx, w_mod, b_mod, g_norm1, g_norm2, w_in, g_cq, g_ckv, w_uq, w_ukv, g_qn_mla, g_kn_mla,
           g_qn_gqa, g_kn_gqa, w_br_mla, w_br_gqa, w_four, w_o, w_up, conv_w, conv_b, w_down):
    params = dict(w_in=w_in, g_norm1=g_norm1, g_norm2=g_norm2, g_cq=g_cq, g_ckv=g_ckv, w_uq=w_uq, w_ukv=w_ukv,
                  g_qn_mla=g_qn_mla, g_kn_mla=g_kn_mla, g_qn_gqa=g_qn_gqa, g_kn_gqa=g_kn_gqa,
                  w_br_mla=w_br_mla, w_br_gqa=w_br_gqa, w_four=w_four, w_o=w_o, w_up=w_up, conv_w=conv_w,
                  conv_b=conv_b, w_down=w_down)
    params["bf16"] = {name: params[name].astype(BF16) for name in STACKED}
    bsz, seq, d = x.shape
    ctx_len = ctx.shape[1]
    depth = w_mod.shape[0]
    tm = 512
    tm_c = min(tm, ctx_len)
    tq = 512

    c_all = jnp.concatenate([c, c_ctx[None], jnp.zeros((SUBLANES - (bsz + 1) % SUBLANES, d), F32)], axis=0)
    mod_all = _mod_call(c_all, w_mod, b_mod)
    tabs = (_rope_tables_mla(seq), _rope_tables_gqa(seq))
    tabs_c = tuple(t[:, :ctx_len] for t in tabs)

    xc = ctx
    for l in range(depth):
        wp = _prep_layer(l, params)
        mod = mod_all[l]
        update_ctx = l < depth - 1
        km, vm, kg, vg, qm, qg, four = _inproj_call(x, mod, wp, tabs, mod_row=None, rope=True, kv_only=False, tm=tm)
        c_out = _inproj_call(xc, mod, wp, tabs_c, mod_row=bsz, rope=False, kv_only=not update_ctx, tm=tm_c)
        kmc, vmc, kgc, vgc = c_out[:4]
        o_mla = _attn_call(qm, [(kmc, vmc), (km, vm)], wp["safe_mla"], wp["gslot"], tabs[0], shared_kv=False, tq=tq, name="attn_mla")
        o_gqa = _attn_call(qg, [(kgc, vgc), (kg, vg)], wp["safe_gqa"], wp["gslot"], tabs[1], shared_kv=True, tq=tq, name="attn_gqa")
        o_four = _fourier_call(four)
        x = _merge_call(x, mod, o_mla, o_gqa, o_four, wp, mod_row=None, tm=tm)
        x = _ffn_call(x, mod, wp, mod_row=None, tm=tm)
        if update_ctx:
            qmc, qgc, fourc = c_out[4:]
            oc_mla = _attn_call(qmc, [(kmc, vmc)], wp["safe_mla"], wp["gslot"], None, shared_kv=False, tq=tm_c, name="attn_mla_ctx")
            oc_gqa = _attn_call(qgc, [(kgc, vgc)], wp["safe_gqa"], wp["gslot"], None, shared_kv=True, tq=tm_c, name="attn_gqa_ctx")
            oc_four = _fourier_call(fourc)
            xc = _merge_call(xc, mod, oc_mla, oc_gqa, oc_four, wp, mod_row=bsz, tm=tm_c)
            xc = _ffn_call(xc, mod, wp, mod_row=bsz, tm=tm_c)
    return x
```

```python
import functools

import numpy as np
import jax
import jax.numpy as jnp
from jax import lax
from jax.experimental import pallas as pl
from jax.experimental.pallas import tpu as pltpu

F32 = jnp.float32
BF16 = jnp.bfloat16

LANES = 128
SUBLANES = 8
VMEM_LIMIT = 56 * 1024 * 1024

GRID_W = 64
ROPE_BASE = 10000.0
EPS = 1e-6
MLA_HEADS = 8
MLA_NOPE = 64
MLA_ROPE = 32
MLA_QK = MLA_NOPE + MLA_ROPE
MLA_V = 64
MLA_Q_RANK = 384
MLA_KV_RANK = 256
GQA_HEADS = 8
GQA_KV_HEADS = 2
GQA_DIM = 64
FOUR_GROUPS = 4
FOUR_GROUP_DIM = 128
FOUR_WIDTH = FOUR_GROUPS * FOUR_GROUP_DIM
N_BRANCH = 3
SUM_LANE = 64
LOG2E = 1.4426950408889634
SAFE_SCORE_BOUND = 40.0
HEADS_PER_STEP = GQA_HEADS // GQA_KV_HEADS

STACKED = ("w_br_mla", "w_br_gqa", "w_four", "w_o", "w_up", "w_down")

C_CKV = 0
C_KROPE = C_CKV + MLA_KV_RANK
C_KG = C_KROPE + LANES
C_VG = C_KG + GQA_KV_HEADS * LANES
C_KV_END = C_VG + GQA_KV_HEADS * LANES
C_CQ = C_KV_END
C_QG = C_CQ + MLA_Q_RANK
C_FOUR = C_QG + GQA_HEADS * LANES
C_END = C_FOUR + FOUR_WIDTH


def _cparams(n_grid, vmem=VMEM_LIMIT):
    return pltpu.CompilerParams(dimension_semantics=("arbitrary",) * n_grid, vmem_limit_bytes=vmem)


def _const_spec(shape):
    nd = len(shape)
    return pl.BlockSpec(shape, lambda *_: (0,) * nd, pipeline_mode=pl.Buffered(1))


def _layer_spec(stack, l):
    return pl.BlockSpec((None,) + stack.shape[1:], lambda *_: (l, 0, 0), pipeline_mode=pl.Buffered(1))


def _rms(x, g, dim):
    ss = jnp.sum(x * x, axis=-1, keepdims=True)
    return x * lax.rsqrt(ss * (1.0 / dim) + EPS) * g


def _modulate(x, g, shift, scale):
    return _rms(x, g, x.shape[-1]) * (1.0 + scale) + shift


def _rope_slot(x, tab_ref, half):
    up = pltpu.roll(x, LANES - half, axis=1)
    dn = pltpu.roll(x, half, axis=1)
    return x * tab_ref[0] + up * tab_ref[1] + dn * tab_ref[2]


def _dot(a, b):
    return jnp.dot(a, b, preferred_element_type=F32)


def _dot_nt(a, bt):
    return lax.dot_general(a, bt, (((1,), (1,)), ((), ())), preferred_element_type=F32)


def _mod_kernel(c_ref, w_ref, b_ref, o_ref):
    c = c_ref[...]
    s = (c * (1.0 / (1.0 + jnp.exp(-c)))).astype(BF16)
    o_ref[...] = _dot(s, w_ref[...].astype(BF16)) + b_ref[...]


def _mod_call(c_all, w_mod, b_mod):
    depth, d, n = w_mod.shape
    tn = n // 4
    return pl.pallas_call(
        _mod_kernel,
        grid=(depth, n // tn),
        in_specs=[pl.BlockSpec(c_all.shape, lambda l, j: (0, 0)),
                  pl.BlockSpec((None, d, tn), lambda l, j: (l, 0, j)),
                  pl.BlockSpec((None, 1, tn), lambda l, j: (l, 0, j))],
        out_specs=pl.BlockSpec((None, c_all.shape[0], tn), lambda l, j: (l, 0, j)),
        out_shape=jax.ShapeDtypeStruct((depth, c_all.shape[0], n), F32),
        compiler_params=_cparams(2),
        name="modulation",
    )(c_all, w_mod, b_mod.reshape(depth, 1, n))


(G_KN_MLA, G_QN_MLA, G_QN_MLA_SW, G_KN_GQA, G_KN_GQA_SW, G_QN_GQA, G_QN_GQA_SW,
 AUG_Q_MLA, AUG_K_MLA, AUG_Q_GQA, AUG_K_GQA, AUG_V) = range(12)
GSLOT_ROWS = 2 * SUBLANES


def _row(ref, r):
    return ref[r:r + 1, :]


def _inv_rms(ss, dim):
    return lax.rsqrt(ss * (1.0 / dim) + EPS)


def _sumsq(x):
    return jnp.sum(x * x, axis=-1, keepdims=True)


def _inproj_kernel(mod_ref, x_ref, g1_ref, w1_ref, gckv_ref, gcq_ref, wukv_ref, wuq_ref, gslot_ref,
                   rm_ref, rg_ref, *out_refs, mod_row, rope, kv_only):
    d = x_ref.shape[-1]
    b = pl.program_id(0) if mod_row is None else mod_row
    shift = mod_ref[pl.ds(b, 1), 0:d]
    scale = mod_ref[pl.ds(b, 1), d:2 * d]
    h = _modulate(x_ref[...], g1_ref[...], shift, scale).astype(BF16)
    p = _dot_nt(h, w1_ref[...])
    km_ref, vm_ref, kg_ref, vg_ref = out_refs[:4]
    aug_v = _row(gslot_ref, AUG_V)

    def gqa_slot(x, g_row, gsw_row, aug_row):
        r = _inv_rms(_sumsq(x), 2 * GQA_DIM)
        if rope:
            y = x * (_row(gslot_ref, g_row) * rg_ref[0]) \
                + pltpu.roll(x, GQA_DIM // 2, axis=1) * (_row(gslot_ref, gsw_row) * rg_ref[1])
        else:
            y = x * _row(gslot_ref, g_row)
        return y * r + _row(gslot_ref, aug_row)

    ckv = _rms(p[:, C_CKV:C_CKV + MLA_KV_RANK], gckv_ref[...], MLA_KV_RANK).astype(BF16)
    kv = _dot(ckv, wukv_ref[...])
    krope = p[:, C_KROPE:C_KROPE + LANES]
    gk = _row(gslot_ref, G_KN_MLA)
    kr = krope * gk
    if rope:
        kr = _rope_slot(kr, rm_ref, MLA_ROPE // 2)
    ss_r = _sumsq(krope)
    for hd in range(MLA_HEADS):
        sl = slice(hd * LANES, (hd + 1) * LANES)
        kh = kv[:, sl]
        r = _inv_rms(_sumsq(kh) + ss_r, MLA_QK)
        km_ref[:, sl] = ((kh * gk + kr) * r + _row(gslot_ref, AUG_K_MLA)).astype(km_ref.dtype)
        vm_ref[hd] = (kv[:, MLA_HEADS * LANES + hd * LANES:MLA_HEADS * LANES + (hd + 1) * LANES]
                      + aug_v).T.astype(vm_ref.dtype)

    for hd in range(GQA_KV_HEADS):
        sl = slice(hd * LANES, (hd + 1) * LANES)
        kh = gqa_slot(p[:, C_KG + hd * LANES:C_KG + (hd + 1) * LANES], G_KN_GQA, G_KN_GQA_SW, AUG_K_GQA)
        kg_ref[:, sl] = kh.astype(kg_ref.dtype)
        vg_ref[hd] = (p[:, C_VG + hd * LANES:C_VG + (hd + 1) * LANES] + aug_v).T.astype(vg_ref.dtype)
    if kv_only:
        return

    qm_ref, qg_ref, four_ref = out_refs[4:]
    cq = _rms(p[:, C_CQ:C_CQ + MLA_Q_RANK], gcq_ref[...], MLA_Q_RANK).astype(BF16)
    qm = _dot(cq, wuq_ref[...])
    for hd in range(MLA_HEADS):
        qm_ref[hd] = qm[:, hd * LANES:(hd + 1) * LANES].T.astype(qm_ref.dtype)
    for hd in range(GQA_HEADS):
        qg_ref[hd] = p[:, C_QG + hd * LANES:C_QG + (hd + 1) * LANES].T.astype(qg_ref.dtype)
    four_ref[...] = p[:, C_FOUR:C_END]


def _inproj_call(x, mod, wp, tabs, *, mod_row, rope, kv_only, tm):
    bsz, seq, d = x.shape
    ncol = C_KV_END if kv_only else C_END
    kernel = functools.partial(_inproj_kernel, mod_row=mod_row, rope=rope, kv_only=kv_only)
    row = lambda width, dt: (pl.BlockSpec((None, tm, width), lambda b, i: (b, i, 0)),
                             jax.ShapeDtypeStruct((bsz, seq, width), dt))
    col = lambda slots: (pl.BlockSpec((None, slots, LANES, tm), lambda b, i: (b, 0, 0, i)),
                         jax.ShapeDtypeStruct((bsz, slots, LANES, seq), BF16))
    outs = [row(MLA_HEADS * LANES, BF16), col(MLA_HEADS), row(GQA_KV_HEADS * LANES, BF16), col(GQA_KV_HEADS)]
    if not kv_only:
        outs += [col(MLA_HEADS), col(GQA_HEADS), row(FOUR_WIDTH, F32)]
    tab = lambda t: pl.BlockSpec((t.shape[0], tm, LANES), lambda b, i: (0, i, 0))
    return pl.pallas_call(
        kernel,
        grid=(bsz, seq // tm),
        in_specs=[_const_spec(mod.shape),
                  pl.BlockSpec((None, tm, d), lambda b, i: (b, i, 0)),
                  _const_spec((1, d)),
                  pl.BlockSpec((ncol, d), lambda b, i: (0, 0), pipeline_mode=pl.Buffered(1)),
                  _const_spec((1, MLA_KV_RANK)), _const_spec((1, MLA_Q_RANK)),
                  _const_spec(wp["w_ukv"].shape), _const_spec(wp["w_uq"].shape),
                  _const_spec(wp["gslot"].shape), tab(tabs[0]), tab(tabs[1])],
        out_specs=[o[0] for o in outs],
        out_shape=[o[1] for o in outs],
        compiler_params=_cparams(2),
        name="inproj_kv" if kv_only else "inproj",
    )(mod, x, wp["g_norm1"], wp["w1"], wp["g_ckv"], wp["g_cq"], wp["w_ukv"], wp["w_uq"], wp["gslot"],
      tabs[0], tabs[1])


def _attn_kernel(tab_ref, q_ref, *refs, n_src, shared_kv, bounded):
    kv_refs, o_ref = refs[:2 * n_src], refs[2 * n_src]
    tq = q_ref.shape[-1]
    if shared_kv:
        const_row, dim, shifts = GQA_DIM, 2 * GQA_DIM, (GQA_DIM // 2,)
    else:
        const_row, dim, shifts = MLA_QK, MLA_QK, (LANES - MLA_ROPE // 2, MLA_ROPE // 2)
    shifts = shifts[:tab_ref.shape[0] - 1]
    is_const = lax.broadcasted_iota(jnp.int32, (LANES, tq), 0) == const_row

    def query(x):
        y = x * tab_ref[0]
        for t, shift in enumerate(shifts):
            y = y + jnp.concatenate([x[LANES - shift:], x[:LANES - shift]], axis=0) * tab_ref[1 + t]
        r = _inv_rms(jnp.sum(x * x, axis=0, keepdims=True), dim)
        return jnp.where(is_const, 1.0, y * r)

    for pair in range(HEADS_PER_STEP // 2):
        halves = []
        for hh in range(2):
            i = pair * 2 + hh
            col = 0 if shared_kv else i * LANES
            vi = 0 if shared_kv else i
            qt = query(q_ref[i].astype(F32)).astype(BF16)
            ss = [_dot(kv_refs[2 * s][:, col:col + LANES], qt) for s in range(n_src)]
            if not bounded:
                m = functools.reduce(jnp.maximum, [jnp.max(s, axis=0, keepdims=True) for s in ss])
                ss = [s - m for s in ss]
            ot = functools.reduce(jnp.add, [_dot(kv_refs[2 * s + 1][vi], jnp.exp2(ss[s]).astype(BF16))
                                            for s in range(n_src)])
            halves.append((ot / ot[SUM_LANE:SUM_LANE + 1, :])[:SUM_LANE])
        o_ref[:, pair * LANES:(pair + 1) * LANES] = jnp.concatenate(halves, axis=0).T.astype(o_ref.dtype)


def _attn_call(q, srcs, safe, tab, *, shared_kv, tq, name):
    bsz, n_heads, _, lq = q.shape
    qcols = HEADS_PER_STEP * LANES
    n_groups = n_heads // HEADS_PER_STEP
    kw = LANES if shared_kv else qcols
    nv = 1 if shared_kv else HEADS_PER_STEP
    in_specs = [pl.BlockSpec((tab.shape[0], LANES, tq), lambda b, g, i: (0, 0, i)),
                pl.BlockSpec((None, HEADS_PER_STEP, LANES, tq), lambda b, g, i: (b, g, 0, i))]
    args = [tab, q]
    for k, v in srcs:
        lk = k.shape[1]
        in_specs += [pl.BlockSpec((None, lk, kw), lambda b, g, i: (b, 0, g)),
                     pl.BlockSpec((None, nv, LANES, lk), lambda b, g, i: (b, g, 0, 0))]
        args += [k, v]
    ow = qcols // 2

    def call(bounded):
        return pl.pallas_call(
            functools.partial(_attn_kernel, n_src=len(srcs), shared_kv=shared_kv, bounded=bounded),
            grid=(bsz, n_groups, lq // tq),
            in_specs=in_specs,
            out_specs=pl.BlockSpec((None, tq, ow), lambda b, g, i: (b, i, g)),
            out_shape=jax.ShapeDtypeStruct((bsz, lq, n_groups * ow), BF16),
            compiler_params=_cparams(3),
            name=name + ("" if bounded else "_rowmax"),
        )

    return lax.cond(safe, lambda *a: call(True)(*a), lambda *a: call(False)(*a), *args)


def _query_tables(gslot, tab, g_row, gsw_row, seq):
    gain = gslot[g_row][:, None]
    if tab is None:
        return jnp.broadcast_to(gain, (LANES, seq))[None]
    gsw = gslot[gsw_row][:, None]
    return jnp.stack([gain * tab[0].T] + [gsw * tab[1 + t].T for t in range(tab.shape[0] - 1)])


def _dft_grid_kernel(ka_ref, mr_ref, mi_ref, cs_ref, tc_ref, ts_ref, x_ref, o_ref, y_ref):
    tile = SUBLANES
    n_a, n_b, width = x_ref.shape
    n_r2 = y_ref.shape[1]
    n_r = n_r2 // 2
    reps = width // LANES
    for j in range(n_b // tile):
        xb = x_ref[:, j * tile:(j + 1) * tile, :].reshape(n_a * tile, width).astype(BF16)
        y = _dot(ka_ref[...], xb)
        y_ref[j * tile:(j + 1) * tile] = y.reshape(tile, n_r2, width)
    for j in range(n_r // tile):
        yr = y_ref[:, j * tile:(j + 1) * tile, :].reshape(n_b * tile, width)
        yi = y_ref[:, n_r + j * tile:n_r + (j + 1) * tile, :].reshape(n_b * tile, width)
        tc = jnp.concatenate([tc_ref[j]] * reps, axis=1)
        ts = jnp.concatenate([ts_ref[j]] * reps, axis=1)
        zr = (yr * tc - yi * ts).astype(BF16)
        zi = (yr * ts + yi * tc).astype(BF16)
        half = mr_ref.shape[0] // 2
        o_re = _dot(mr_ref[:half], zr) + _dot(mi_ref[:half], zi)
        o_im = _dot(mr_ref[half:], zr) + _dot(mi_ref[half:], zi)
        ocat = jnp.concatenate([o_re, o_im], axis=1).astype(BF16)
        res = _dot(ocat, cs_ref[...])
        o_ref[:, j * tile:(j + 1) * tile, :] = res.reshape(half // tile, tile, width)


def _dft_small_kernel(f_ref, cs_ref, x_ref, o_ref):
    x2 = _dot(f_ref[...], x_ref[...].astype(BF16))
    half = x2.shape[0] // 2
    ocat = jnp.concatenate([x2[:half], x2[half:]], axis=1).astype(BF16)
    o_ref[...] = _dot(ocat, cs_ref[...])


def _bf16_const(a):
    return jnp.asarray(np.asarray(a, np.float32), F32).astype(BF16)


def _dft_cos_sin(n):
    idx = np.arange(n)
    ang = 2.0 * np.pi * ((idx[:, None] * idx[None, :]) % n) / n
    return np.cos(ang), np.sin(ang)


def _channel_dft_const(seq, groups):
    c, s = _dft_cos_sin(FOUR_GROUP_DIM)
    eye = np.eye(groups)
    norm = 1.0 / np.sqrt(seq * FOUR_GROUP_DIM)
    return _bf16_const(np.concatenate([np.kron(eye, c), np.kron(eye, s)], axis=0) * norm)


def _fourier_call(f):
    bsz, seq, w = f.shape
    if seq < GRID_W * SUBLANES:
        c, s = _dft_cos_sin(seq)
        fmat = _bf16_const(np.concatenate([c, -s], axis=0))
        cs = _channel_dft_const(seq, FOUR_GROUPS)
        return pl.pallas_call(
            _dft_small_kernel,
            grid=(bsz,),
            in_specs=[_const_spec(fmat.shape), _const_spec(cs.shape),
                      pl.BlockSpec((None, seq, w), lambda b: (b, 0, 0))],
            out_specs=pl.BlockSpec((None, seq, w), lambda b: (b, 0, 0)),
            out_shape=jax.ShapeDtypeStruct((bsz, seq, w), F32),
            compiler_params=_cparams(1),
            name="dft_small",
        )(fmat, cs, f)

    rows = seq // GRID_W
    n_split = 2
    wh = w // n_split
    tile = SUBLANES
    eye = np.eye(tile)
    ca, sa = _dft_cos_sin(rows)
    fa = np.concatenate([ca, -sa], axis=0)
    ka = _bf16_const(np.einsum("ra,ij->iraj", fa, eye).reshape(tile * 2 * rows, rows * tile))
    cb, sb = _dft_cos_sin(GRID_W)
    expand = lambda m: np.einsum("pb,ij->pibj", m, eye).reshape(GRID_W * tile, GRID_W * tile)
    mr = _bf16_const(np.concatenate([expand(cb), expand(-sb)], axis=0))
    mi = _bf16_const(np.concatenate([expand(sb), expand(cb)], axis=0))
    cs = _channel_dft_const(seq, FOUR_GROUPS // n_split)
    ang = 2.0 * np.pi * (np.arange(GRID_W)[:, None] * np.arange(rows)[None, :]) / seq
    tw = lambda t: jnp.asarray(np.repeat(
        t.reshape(GRID_W, rows // tile, tile).transpose(1, 0, 2).reshape(rows // tile, GRID_W * tile)[:, :, None],
        LANES, axis=2), F32)
    tc, ts = tw(np.cos(ang)), tw(-np.sin(ang))
    o = pl.pallas_call(
        _dft_grid_kernel,
        grid=(bsz, n_split),
        in_specs=[_const_spec(ka.shape), _const_spec(mr.shape), _const_spec(mi.shape), _const_spec(cs.shape),
                  _const_spec(tc.shape), _const_spec(ts.shape),
                  pl.BlockSpec((None, rows, GRID_W, wh), lambda b, j: (b, 0, 0, j))],
        out_specs=pl.BlockSpec((None, GRID_W, rows, wh), lambda b, j: (b, 0, 0, j)),
        out_shape=jax.ShapeDtypeStruct((bsz, GRID_W, rows, w), F32),
        scratch_shapes=[pltpu.VMEM((GRID_W, 2 * rows, wh), F32)],
        compiler_params=_cparams(2),
        name="dft_grid",
    )(ka, mr, mi, cs, tc, ts, f.reshape(bsz, rows, GRID_W, w))
    return o.reshape(bsz, seq, w)


def _merge_kernel(mod_ref, x_ref, g1_ref, om_ref, og_ref, of_ref, wg_ref, wbm_ref, wbg_ref, wbf_ref, wo_ref,
                  o_ref, *, mod_row):
    d = x_ref.shape[-1]
    b = pl.program_id(0) if mod_row is None else mod_row
    shift = mod_ref[pl.ds(b, 1), 0:d]
    scale = mod_ref[pl.ds(b, 1), d:2 * d]
    gate1 = mod_ref[pl.ds(b, 1), 2 * d:3 * d]
    x = x_ref[...]
    h = _modulate(x, g1_ref[...], shift, scale).astype(BF16)
    y = None
    branches = ((om_ref[...], wbm_ref), (og_ref[...], wbg_ref), (of_ref[...].astype(BF16), wbf_ref))
    for n, (o, w_ref) in enumerate(branches):
        gate = 1.0 / (1.0 + jnp.exp(-_dot_nt(h, wg_ref[n * d:(n + 1) * d, :])))
        t = gate * _dot(o, w_ref[...])
        y = t if y is None else y + t
    o_ref[...] = x + gate1 * _dot(y.astype(BF16), wo_ref[...])


def _merge_call(x, mod, o_mla, o_gqa, o_four, wp, *, mod_row, tm):
    bsz, seq, d = x.shape
    row = lambda width: pl.BlockSpec((None, tm, width), lambda b, i: (b, i, 0))
    return pl.pallas_call(
        functools.partial(_merge_kernel, mod_row=mod_row),
        grid=(bsz, seq // tm),
        in_specs=[_const_spec(mod.shape), row(d), _const_spec((1, d)),
                  row(o_mla.shape[-1]), row(o_gqa.shape[-1]), row(o_four.shape[-1]),
                  _const_spec(wp["w_gate"].shape), _layer_spec(wp["w_br_mla"], wp["layer"]),
                  _layer_spec(wp["w_br_gqa"], wp["layer"]), _layer_spec(wp["w_four"], wp["layer"]),
                  _layer_spec(wp["w_o"], wp["layer"])],
        out_specs=row(d),
        out_shape=jax.ShapeDtypeStruct(x.shape, F32),
        compiler_params=_cparams(2),
        name="merge",
    )(mod, x, wp["g_norm1"], o_mla, o_gqa, o_four, wp["w_gate"], wp["w_br_mla"], wp["w_br_gqa"],
      wp["w_four"], wp["w_o"])


def _ffn_kernel(mod_ref, xp_ref, x_ref, xn_ref, g2_ref, wup_ref, cw_ref, cb_ref, wdn_ref, o_ref, *, mod_row):
    d = x_ref.shape[-1]
    tm = x_ref.shape[0]
    halo = xp_ref.shape[0]
    b = pl.program_id(0) if mod_row is None else mod_row
    i = pl.program_id(1)
    shift = mod_ref[pl.ds(b, 1), 3 * d:4 * d]
    scale = mod_ref[pl.ds(b, 1), 4 * d:5 * d]
    gate2 = mod_ref[pl.ds(b, 1), 5 * d:6 * d]
    x = x_ref[...]
    xe = jnp.concatenate([xp_ref[...], x, xn_ref[...]], axis=0)
    he = _modulate(xe, g2_ref[...], shift, scale)
    rid = lax.broadcasted_iota(jnp.int32, (tm + 2 * halo, 1), 0)
    outside = ((rid < halo) & (i == 0)) | ((rid >= tm + halo) & (i == pl.num_programs(1) - 1))
    he = jnp.where(outside, 0.0, he).astype(BF16)
    u = _dot(he, wup_ref[...])
    n = tm + 2 * halo
    up = pltpu.roll(u, 1, axis=0)[halo:halo + tm]
    dn = pltpu.roll(u, n - 1, axis=0)[halo:halo + tm]
    u = cw_ref[0:1, :] * up + cw_ref[1:2, :] * u[halo:halo + tm] + cw_ref[2:3, :] * dn + cb_ref[...]
    ff = u.shape[-1] // 2
    a, v = u[:, :ff], u[:, ff:]
    act = (a * (1.0 / (1.0 + jnp.exp(-a))) * v).astype(BF16)
    o_ref[...] = x + gate2 * _dot(act, wdn_ref[...])


def _ffn_call(x, mod, wp, *, mod_row, tm):
    bsz, seq, d = x.shape
    halo = SUBLANES
    nblk = seq // halo
    per = tm // halo
    return pl.pallas_call(
        functools.partial(_ffn_kernel, mod_row=mod_row),
        grid=(bsz, seq // tm),
        in_specs=[_const_spec(mod.shape),
                  pl.BlockSpec((None, halo, d), lambda b, i: (b, jnp.maximum(i * per - 1, 0), 0)),
                  pl.BlockSpec((None, tm, d), lambda b, i: (b, i, 0)),
                  pl.BlockSpec((None, halo, d), lambda b, i: (b, jnp.minimum((i + 1) * per, nblk - 1), 0)),
                  _const_spec((1, d)), _layer_spec(wp["w_up"], wp["layer"]), _const_spec(wp["conv_w"].shape),
                  _const_spec(wp["conv_b"].shape), _layer_spec(wp["w_down"], wp["layer"])],
        out_specs=pl.BlockSpec((None, tm, d), lambda b, i: (b, i, 0)),
        out_shape=jax.ShapeDtypeStruct(x.shape, F32),
        compiler_params=_cparams(2),
        name="ffn",
    )(mod, x, x, x, wp["g_norm2"], wp["w_up"], wp["conv_w"], wp["conv_b"], wp["w_down"])


def _slots(w, n_slots, width, offset=0):
    k = w.shape[0]
    w3 = w.reshape(k, n_slots, width)
    w3 = jnp.pad(w3, ((0, 0), (0, 0), (offset, LANES - offset - width)))
    return w3.reshape(k, n_slots * LANES)


def _slots_t(wt, n_slots, width, offset=0):
    k = wt.shape[1]
    w3 = wt.reshape(n_slots, width, k)
    w3 = jnp.pad(w3, ((0, 0), (offset, LANES - offset - width), (0, 0)))
    return w3.reshape(n_slots * LANES, k)


def _prep_layer(l, p):
    d = p["w_in"].shape[1]
    w_t = jnp.swapaxes(p["w_in"], 1, 2)[l].astype(BF16)
    o = 0
    c_kv = w_t[o:o + MLA_KV_RANK]; o += MLA_KV_RANK
    k_rope = w_t[o:o + MLA_ROPE]; o += MLA_ROPE
    k_g = w_t[o:o + GQA_KV_HEADS * GQA_DIM]; o += GQA_KV_HEADS * GQA_DIM
    v_g = w_t[o:o + GQA_KV_HEADS * GQA_DIM]; o += GQA_KV_HEADS * GQA_DIM
    c_q = w_t[o:o + MLA_Q_RANK]; o += MLA_Q_RANK
    q_g = w_t[o:o + GQA_HEADS * GQA_DIM]; o += GQA_HEADS * GQA_DIM
    four = w_t[o:o + FOUR_WIDTH]; o += FOUR_WIDTH
    gates = w_t[o:o + N_BRANCH * d]
    dup = lambda w, n: _slots_t(w, n, GQA_DIM, 0) + _slots_t(w, n, GQA_DIM, GQA_DIM)
    w1 = jnp.concatenate([
        c_kv,
        _slots_t(k_rope, 1, MLA_ROPE, MLA_NOPE),
        dup(k_g, GQA_KV_HEADS),
        _slots_t(v_g, GQA_KV_HEADS, GQA_DIM),
        c_q,
        dup(q_g, GQA_HEADS),
        four], axis=0)

    w_ukv = p["w_ukv"][l].reshape(MLA_KV_RANK, MLA_HEADS, MLA_NOPE + MLA_V)
    k_nope = w_ukv[:, :, :MLA_NOPE].reshape(MLA_KV_RANK, MLA_HEADS * MLA_NOPE)
    v_mla = w_ukv[:, :, MLA_NOPE:].reshape(MLA_KV_RANK, MLA_HEADS * MLA_V)
    w_ukv2 = jnp.concatenate([_slots(k_nope, MLA_HEADS, MLA_NOPE), _slots(v_mla, MLA_HEADS, MLA_V)],
                             axis=1).astype(BF16)
    w_uq2 = _slots(p["w_uq"][l], MLA_HEADS, MLA_QK).astype(BF16)

    pad = lambda g: jnp.pad(g, (0, LANES - g.shape[0]))
    onehot = lambda lane: jnp.zeros((LANES,), F32).at[lane].set(1.0)
    bound_m = (MLA_QK ** 0.5) * jnp.max(jnp.abs(p["g_qn_mla"][l])) * jnp.max(jnp.abs(p["g_kn_mla"][l])) * (1 + 2.0 ** -6)
    bound_g = (GQA_DIM ** 0.5) * jnp.max(jnp.abs(p["g_qn_gqa"][l])) * jnp.max(jnp.abs(p["g_kn_gqa"][l])) * (1 + 2.0 ** -6)
    safe_m, safe_g = bound_m <= SAFE_SCORE_BOUND, bound_g <= SAFE_SCORE_BOUND
    swap = lambda g, lo, n: jnp.concatenate([g[:lo], g[lo + n:lo + 2 * n], g[lo:lo + n], g[lo + 2 * n:]])
    c_m, c_g = MLA_QK ** -0.5 * LOG2E, GQA_DIM ** -0.5 * LOG2E
    rope_only = (jnp.arange(LANES) >= MLA_NOPE).astype(F32)
    rows = [None] * 12
    rows[G_KN_MLA] = pad(p["g_kn_mla"][l])
    rows[G_QN_MLA] = pad(p["g_qn_mla"][l]) * c_m
    rows[G_QN_MLA_SW] = swap(pad(p["g_qn_mla"][l]), MLA_NOPE, MLA_ROPE // 2) * rope_only * c_m
    rows[G_KN_GQA] = pad(p["g_kn_gqa"][l])
    rows[G_KN_GQA_SW] = swap(pad(p["g_kn_gqa"][l]), 0, GQA_DIM // 2)
    rows[G_QN_GQA] = pad(p["g_qn_gqa"][l]) * c_g
    rows[G_QN_GQA_SW] = swap(pad(p["g_qn_gqa"][l]), 0, GQA_DIM // 2) * c_g
    rows[AUG_Q_MLA] = onehot(MLA_QK)
    rows[AUG_K_MLA] = onehot(MLA_QK) * jnp.where(safe_m, -bound_m * LOG2E, 0.0)
    rows[AUG_Q_GQA] = onehot(GQA_DIM)
    rows[AUG_K_GQA] = onehot(GQA_DIM) * jnp.where(safe_g, -bound_g * LOG2E, 0.0)
    rows[AUG_V] = onehot(SUM_LANE)
    gslot = jnp.stack(rows)
    gslot = jnp.pad(gslot, ((0, GSLOT_ROWS - gslot.shape[0]), (0, 0)))
    return {
        "w1": w1, "w_ukv": w_ukv2, "w_uq": w_uq2, "gslot": gslot, "safe_mla": safe_m, "safe_gqa": safe_g,
        "g_norm1": p["g_norm1"][l][None], "g_norm2": p["g_norm2"][l][None],
        "g_ckv": p["g_ckv"][l][None], "g_cq": p["g_cq"][l][None],
        "w_gate": gates, "conv_w": p["conv_w"][l], "conv_b": p["conv_b"][l][None],
        "layer": l, **{name: p["bf16"][name] for name in STACKED},
    }


def _rope_angles(seq, rot_dim):
    rows = seq // GRID_W
    row = np.repeat(np.arange(rows, dtype=np.float32), GRID_W)
    col = np.tile(np.arange(GRID_W, dtype=np.float32), rows)
    n_f = rot_dim // 4
    inv = (ROPE_BASE ** (-np.arange(n_f, dtype=np.float32) / n_f)).astype(np.float32)
    ang = jnp.asarray(np.concatenate([row[:, None] * inv, col[:, None] * inv], axis=-1), F32)
    return jnp.cos(ang), jnp.sin(ang)


def _rope_tables_mla(seq):
    cos, sin = _rope_angles(seq, MLA_ROPE)
    half = MLA_ROPE // 2
    z = lambda n: jnp.zeros((seq, n), F32)
    o = lambda n: jnp.ones((seq, n), F32)
    rest = LANES - MLA_QK
    return jnp.stack([
        jnp.concatenate([o(MLA_NOPE), cos, cos, o(rest)], axis=1),
        jnp.concatenate([z(MLA_NOPE), -sin, z(half), z(rest)], axis=1),
        jnp.concatenate([z(MLA_NOPE), z(half), sin, z(rest)], axis=1)])


def _rope_tables_gqa(seq):
    cos, sin = _rope_angles(seq, GQA_DIM)
    z = jnp.zeros((seq, LANES - GQA_DIM), F32)
    return jnp.stack([jnp.concatenate([cos, cos, z], axis=1), jnp.concatenate([-sin, sin, z], axis=1)])


def kernel(x, c, ctx, c_ctx, w_mod, b_mod, g_norm1, g_norm2, w_in, g_cq, g_ckv, w_uq, w_ukv, g_qn_mla, g_kn_mla,
           g_qn_gqa, g_kn_gqa, w_br_mla, w_br_gqa, w_four, w_o, w_up, conv_w, conv_b, w_down):
    params = dict(w_in=w_in, g_norm1=g_norm1, g_norm2=g_norm2, g_cq=g_cq, g_ckv=g_ckv, w_uq=w_uq, w_ukv=w_ukv,
                  g_qn_mla=g_qn_mla, g_kn_mla=g_kn_mla, g_qn_gqa=g_qn_gqa, g_kn_gqa=g_kn_gqa,
                  w_br_mla=w_br_mla, w_br_gqa=w_br_gqa, w_four=w_four, w_o=w_o, w_up=w_up, conv_w=conv_w,
                  conv_b=conv_b, w_down=w_down)
    params["bf16"] = {name: params[name].astype(BF16) for name in STACKED}
    bsz, seq, d = x.shape
    ctx_len = ctx.shape[1]
    depth = w_mod.shape[0]
    tm = 512
    tm_c = min(tm, ctx_len)
    tq = 512

    c_all = jnp.concatenate([c, c_ctx[None], jnp.zeros((SUBLANES - (bsz + 1) % SUBLANES, d), F32)], axis=0)
    mod_all = _mod_call(c_all, w_mod, b_mod)
    tabs = (_rope_tables_mla(seq), _rope_tables_gqa(seq))
    tabs_c = tuple(t[:, :ctx_len] for t in tabs)

    xc = ctx
    for l in range(depth):
        wp = _prep_layer(l, params)
        mod = mod_all[l]
        update_ctx = l < depth - 1
        km, vm, kg, vg, qm, qg, four = _inproj_call(x, mod, wp, tabs, mod_row=None, rope=True, kv_only=False, tm=tm)
        c_out = _inproj_call(xc, mod, wp, tabs_c, mod_row=bsz, rope=False, kv_only=not update_ctx, tm=tm_c)
        kmc, vmc, kgc, vgc = c_out[:4]
        qtab_m = _query_tables(wp["gslot"], tabs[0], G_QN_MLA, G_QN_MLA_SW, seq)
        qtab_g = _query_tables(wp["gslot"], tabs[1], G_QN_GQA, G_QN_GQA_SW, seq)
        o_mla = _attn_call(qm, [(kmc, vmc), (km, vm)], wp["safe_mla"], qtab_m, shared_kv=False, tq=tq, name="attn_mla")
        o_gqa = _attn_call(qg, [(kgc, vgc), (kg, vg)], wp["safe_gqa"], qtab_g, shared_kv=True, tq=tq, name="attn_gqa")
        o_four = _fourier_call(four)
        x = _merge_call(x, mod, o_mla, o_gqa, o_four, wp, mod_row=None, tm=tm)
        x = _ffn_call(x, mod, wp, mod_row=None, tm=tm)
        if update_ctx:
            qmc, qgc, fourc = c_out[4:]
            qtab_mc = _query_tables(wp["gslot"], None, G_QN_MLA, G_QN_MLA_SW, ctx_len)
            qtab_gc = _query_tables(wp["gslot"], None, G_QN_GQA, G_QN_GQA_SW, ctx_len)
            oc_mla = _attn_call(qmc, [(kmc, vmc)], wp["safe_mla"], qtab_mc, shared_kv=False, tq=tm_c, name="attn_mla_ctx")
            oc_gqa = _attn_call(qgc, [(kgc, vgc)], wp["safe_gqa"], qtab_gc, shared_kv=True, tq=tm_c, name="attn_gqa_ctx")
            oc_four = _fourier_call(fourc)
            xc = _merge_call(xc, mod, oc_mla, oc_gqa, oc_four, wp, mod_row=bsz, tm=tm_c)
            xc = _ffn_call(xc, mod, wp, mod_row=bsz, tm=tm_c)
    return x
```

```python
import functools

import numpy as np
import jax
import jax.numpy as jnp
from jax import lax
from jax.experimental import pallas as pl
from jax.experimental.pallas import tpu as pltpu

F32 = jnp.float32
BF16 = jnp.bfloat16

LANES = 128
SUBLANES = 8
VMEM_LIMIT = 56 * 1024 * 1024

GRID_W = 64
ROPE_BASE = 10000.0
EPS = 1e-6
MLA_HEADS = 8
MLA_NOPE = 64
MLA_ROPE = 32
MLA_QK = MLA_NOPE + MLA_ROPE
MLA_V = 64
MLA_Q_RANK = 384
MLA_KV_RANK = 256
GQA_HEADS = 8
GQA_KV_HEADS = 2
GQA_DIM = 64
FOUR_GROUPS = 4
FOUR_GROUP_DIM = 128
FOUR_WIDTH = FOUR_GROUPS * FOUR_GROUP_DIM
N_BRANCH = 3
SUM_LANE = 64
LOG2E = 1.4426950408889634
SAFE_SCORE_BOUND = 40.0
HEADS_PER_STEP = GQA_HEADS // GQA_KV_HEADS

STACKED = ("w_br_mla", "w_br_gqa", "w_four", "w_o", "w_up", "w_down")

C_CKV = 0
C_KROPE = C_CKV + MLA_KV_RANK
C_KG = C_KROPE + LANES
C_VG = C_KG + GQA_KV_HEADS * LANES
C_KV_END = C_VG + GQA_KV_HEADS * LANES
C_CQ = C_KV_END
C_QG = C_CQ + MLA_Q_RANK
C_FOUR = C_QG + GQA_HEADS * LANES
C_END = C_FOUR + FOUR_WIDTH


def _cparams(n_grid, vmem=VMEM_LIMIT):
    return pltpu.CompilerParams(dimension_semantics=("arbitrary",) * n_grid, vmem_limit_bytes=vmem)


def _const_spec(shape):
    nd = len(shape)
    return pl.BlockSpec(shape, lambda *_: (0,) * nd, pipeline_mode=pl.Buffered(1))


def _layer_spec(stack, l):
    return pl.BlockSpec((None,) + stack.shape[1:], lambda *_: (l, 0, 0), pipeline_mode=pl.Buffered(1))


def _rms(x, g, dim):
    ss = jnp.sum(x * x, axis=-1, keepdims=True)
    return x * lax.rsqrt(ss * (1.0 / dim) + EPS) * g


def _modulate(x, g, shift, scale):
    return _rms(x, g, x.shape[-1]) * (1.0 + scale) + shift


def _rope_slot(x, tab_ref, half):
    up = pltpu.roll(x, LANES - half, axis=1)
    dn = pltpu.roll(x, half, axis=1)
    return x * tab_ref[0] + up * tab_ref[1] + dn * tab_ref[2]


def _dot(a, b):
    return jnp.dot(a, b, preferred_element_type=F32)


def _dot_nt(a, bt):
    return lax.dot_general(a, bt, (((1,), (1,)), ((), ())), preferred_element_type=F32)


def _mod_kernel(c_ref, w_ref, b_ref, o_ref):
    c = c_ref[...]
    s = (c * (1.0 / (1.0 + jnp.exp(-c)))).astype(BF16)
    o_ref[...] = _dot(s, w_ref[...].astype(BF16)) + b_ref[...]


def _mod_call(c_all, w_mod, b_mod):
    depth, d, n = w_mod.shape
    tn = n // 4
    return pl.pallas_call(
        _mod_kernel,
        grid=(depth, n // tn),
        in_specs=[pl.BlockSpec(c_all.shape, lambda l, j: (0, 0)),
                  pl.BlockSpec((None, d, tn), lambda l, j: (l, 0, j)),
                  pl.BlockSpec((None, 1, tn), lambda l, j: (l, 0, j))],
        out_specs=pl.BlockSpec((None, c_all.shape[0], tn), lambda l, j: (l, 0, j)),
        out_shape=jax.ShapeDtypeStruct((depth, c_all.shape[0], n), F32),
        compiler_params=_cparams(2),
        name="modulation",
    )(c_all, w_mod, b_mod.reshape(depth, 1, n))


(G_KN_MLA, G_QN_MLA, G_QN_MLA_SW, G_KN_GQA, G_KN_GQA_SW, G_QN_GQA, G_QN_GQA_SW,
 AUG_K_MLA, AUG_K_GQA, AUG_V) = range(10)
GSLOT_ROWS = 2 * SUBLANES


def _row(ref, r):
    return ref[r:r + 1, :]


def _inv_rms(ss, dim):
    return lax.rsqrt(ss * (1.0 / dim) + EPS)


def _sumsq(x):
    return jnp.sum(x * x, axis=-1, keepdims=True)


def _inproj_kernel(mod_ref, x_ref, g1_ref, w1_ref, gckv_ref, gcq_ref, wukv_ref, wuq_ref, gslot_ref,
                   rm_ref, rg_ref, *out_refs, mod_row, rope, kv_only):
    d = x_ref.shape[-1]
    b = pl.program_id(0) if mod_row is None else mod_row
    shift = mod_ref[pl.ds(b, 1), 0:d]
    scale = mod_ref[pl.ds(b, 1), d:2 * d]
    h = _modulate(x_ref[...], g1_ref[...], shift, scale).astype(BF16)
    p = _dot_nt(h, w1_ref[...])
    km_ref, vm_ref, kg_ref, vg_ref = out_refs[:4]
    aug_v = _row(gslot_ref, AUG_V)

    def gqa_slot(x, g_row, gsw_row, aug_row):
        r = _inv_rms(_sumsq(x), 2 * GQA_DIM)
        if rope:
            y = x * (_row(gslot_ref, g_row) * rg_ref[0]) \
                + pltpu.roll(x, GQA_DIM // 2, axis=1) * (_row(gslot_ref, gsw_row) * rg_ref[1])
        else:
            y = x * _row(gslot_ref, g_row)
        return y * r + _row(gslot_ref, aug_row)

    ckv = _rms(p[:, C_CKV:C_CKV + MLA_KV_RANK], gckv_ref[...], MLA_KV_RANK).astype(BF16)
    kv = _dot(ckv, wukv_ref[...])
    krope = p[:, C_KROPE:C_KROPE + LANES]
    gk = _row(gslot_ref, G_KN_MLA)
    kr = krope * gk
    if rope:
        kr = _rope_slot(kr, rm_ref, MLA_ROPE // 2)
    ss_r = _sumsq(krope)
    for hd in range(MLA_HEADS):
        sl = slice(hd * LANES, (hd + 1) * LANES)
        kh = kv[:, sl]
        r = _inv_rms(_sumsq(kh) + ss_r, MLA_QK)
        km_ref[:, sl] = ((kh * gk + kr) * r + _row(gslot_ref, AUG_K_MLA)).astype(km_ref.dtype)
        vm_ref[hd] = (kv[:, MLA_HEADS * LANES + hd * LANES:MLA_HEADS * LANES + (hd + 1) * LANES]
                      + aug_v).T.astype(vm_ref.dtype)

    for hd in range(GQA_KV_HEADS):
        sl = slice(hd * LANES, (hd + 1) * LANES)
        kh = gqa_slot(p[:, C_KG + hd * LANES:C_KG + (hd + 1) * LANES], G_KN_GQA, G_KN_GQA_SW, AUG_K_GQA)
        kg_ref[:, sl] = kh.astype(kg_ref.dtype)
        vg_ref[hd] = (p[:, C_VG + hd * LANES:C_VG + (hd + 1) * LANES] + aug_v).T.astype(vg_ref.dtype)
    if kv_only:
        return

    qm_ref, qg_ref, four_ref = out_refs[4:]
    cq = _rms(p[:, C_CQ:C_CQ + MLA_Q_RANK], gcq_ref[...], MLA_Q_RANK).astype(BF16)
    qm = _dot(cq, wuq_ref[...])
    for hd in range(MLA_HEADS):
        qm_ref[hd] = qm[:, hd * LANES:(hd + 1) * LANES].T.astype(qm_ref.dtype)
    for hd in range(GQA_HEADS):
        qg_ref[hd] = p[:, C_QG + hd * LANES:C_QG + (hd + 1) * LANES].T.astype(qg_ref.dtype)
    four_ref[...] = p[:, C_FOUR:C_END]


def _inproj_call(x, mod, wp, tabs, *, mod_row, rope, kv_only, tm):
    bsz, seq, d = x.shape
    ncol = C_KV_END if kv_only else C_END
    kernel = functools.partial(_inproj_kernel, mod_row=mod_row, rope=rope, kv_only=kv_only)
    row = lambda width, dt: (pl.BlockSpec((None, tm, width), lambda b, i: (b, i, 0)),
                             jax.ShapeDtypeStruct((bsz, seq, width), dt))
    col = lambda slots: (pl.BlockSpec((None, slots, LANES, tm), lambda b, i: (b, 0, 0, i)),
                         jax.ShapeDtypeStruct((bsz, slots, LANES, seq), BF16))
    outs = [row(MLA_HEADS * LANES, BF16), col(MLA_HEADS), row(GQA_KV_HEADS * LANES, BF16), col(GQA_KV_HEADS)]
    if not kv_only:
        outs += [col(MLA_HEADS), col(GQA_HEADS), row(FOUR_WIDTH, F32)]
    tab = lambda t: pl.BlockSpec((t.shape[0], tm, LANES), lambda b, i: (0, i, 0))
    return pl.pallas_call(
        kernel,
        grid=(bsz, seq // tm),
        in_specs=[_const_spec(mod.shape),
                  pl.BlockSpec((None, tm, d), lambda b, i: (b, i, 0)),
                  _const_spec((1, d)),
                  pl.BlockSpec((ncol, d), lambda b, i: (0, 0), pipeline_mode=pl.Buffered(1)),
                  _const_spec((1, MLA_KV_RANK)), _const_spec((1, MLA_Q_RANK)),
                  _const_spec(wp["w_ukv"].shape), _const_spec(wp["w_uq"].shape),
                  _const_spec(wp["gslot"].shape), tab(tabs[0]), tab(tabs[1])],
        out_specs=[o[0] for o in outs],
        out_shape=[o[1] for o in outs],
        compiler_params=_cparams(2),
        name="inproj_kv" if kv_only else "inproj",
    )(mod, x, wp["g_norm1"], wp["w1"], wp["g_ckv"], wp["g_cq"], wp["w_ukv"], wp["w_uq"], wp["gslot"],
      tabs[0], tabs[1])


def _attn_kernel(tab_ref, q_ref, *refs, n_src, shared_kv, bounded):
    kv_refs, o_ref = refs[:2 * n_src], refs[2 * n_src]
    tq = q_ref.shape[-1]
    if shared_kv:
        const_row, dim, shifts = GQA_DIM, 2 * GQA_DIM, (GQA_DIM // 2,)
    else:
        const_row, dim, shifts = MLA_QK, MLA_QK, (LANES - MLA_ROPE // 2, MLA_ROPE // 2)
    shifts = shifts[:tab_ref.shape[0] - 1]
    is_const = lax.broadcasted_iota(jnp.int32, (LANES, tq), 0) == const_row

    def query(x):
        y = x * tab_ref[0]
        for t, shift in enumerate(shifts):
            y = y + jnp.concatenate([x[LANES - shift:], x[:LANES - shift]], axis=0) * tab_ref[1 + t]
        r = _inv_rms(jnp.sum(x * x, axis=0, keepdims=True), dim)
        return jnp.where(is_const, 1.0, y * r)

    for pair in range(HEADS_PER_STEP // 2):
        halves = []
        for hh in range(2):
            i = pair * 2 + hh
            col = 0 if shared_kv else i * LANES
            vi = 0 if shared_kv else i
            qt = query(q_ref[i].astype(F32)).astype(BF16)
            ss = [_dot(kv_refs[2 * s][:, col:col + LANES], qt) for s in range(n_src)]
            if not bounded:
                m = functools.reduce(jnp.maximum, [jnp.max(s, axis=0, keepdims=True) for s in ss])
                ss = [s - m for s in ss]
            ot = functools.reduce(jnp.add, [_dot(kv_refs[2 * s + 1][vi], jnp.exp2(ss[s]).astype(BF16))
                                            for s in range(n_src)])
            halves.append((ot / ot[SUM_LANE:SUM_LANE + 1, :])[:SUM_LANE])
        o_ref[:, pair * LANES:(pair + 1) * LANES] = jnp.concatenate(halves, axis=0).T.astype(o_ref.dtype)


def _attn_call(q, srcs, safe, tab, *, shared_kv, tq, name):
    bsz, n_heads, _, lq = q.shape
    qcols = HEADS_PER_STEP * LANES
    n_groups = n_heads // HEADS_PER_STEP
    kw = LANES if shared_kv else qcols
    nv = 1 if shared_kv else HEADS_PER_STEP
    in_specs = [pl.BlockSpec((tab.shape[0], LANES, tq), lambda b, g, i: (0, 0, i)),
                pl.BlockSpec((None, HEADS_PER_STEP, LANES, tq), lambda b, g, i: (b, g, 0, i))]
    args = [tab, q]
    for k, v in srcs:
        lk = k.shape[1]
        in_specs += [pl.BlockSpec((None, lk, kw), lambda b, g, i: (b, 0, g)),
                     pl.BlockSpec((None, nv, LANES, lk), lambda b, g, i: (b, g, 0, 0))]
        args += [k, v]
    ow = qcols // 2

    def call(bounded):
        return pl.pallas_call(
            functools.partial(_attn_kernel, n_src=len(srcs), shared_kv=shared_kv, bounded=bounded),
            grid=(bsz, n_groups, lq // tq),
            in_specs=in_specs,
            out_specs=pl.BlockSpec((None, tq, ow), lambda b, g, i: (b, i, g)),
            out_shape=jax.ShapeDtypeStruct((bsz, lq, n_groups * ow), BF16),
            compiler_params=_cparams(3),
            name=name + ("" if bounded else "_rowmax"),
        )

    return lax.cond(safe, lambda *a: call(True)(*a), lambda *a: call(False)(*a), *args)


def _query_tables(gslot, tab, g_row, gsw_row, seq):
    gain = gslot[g_row][:, None]
    if tab is None:
        return jnp.broadcast_to(gain, (LANES, seq))[None]
    gsw = gslot[gsw_row][:, None]
    return jnp.stack([gain * tab[0].T] + [gsw * tab[1 + t].T for t in range(tab.shape[0] - 1)])


def _dft_grid_kernel(ka_ref, mr_ref, mi_ref, cs_ref, tc_ref, ts_ref, x_ref, o_ref, y_ref):
    tile = SUBLANES
    n_a, n_b, width = x_ref.shape
    n_r2 = y_ref.shape[1]
    n_r = n_r2 // 2
    reps = width // LANES
    for j in range(n_b // tile):
        xb = x_ref[:, j * tile:(j + 1) * tile, :].reshape(n_a * tile, width).astype(BF16)
        y = _dot(ka_ref[...], xb)
        y_ref[j * tile:(j + 1) * tile] = y.reshape(tile, n_r2, width)
    for j in range(n_r // tile):
        yr = y_ref[:, j * tile:(j + 1) * tile, :].reshape(n_b * tile, width)
        yi = y_ref[:, n_r + j * tile:n_r + (j + 1) * tile, :].reshape(n_b * tile, width)
        tc = jnp.concatenate([tc_ref[j]] * reps, axis=1)
        ts = jnp.concatenate([ts_ref[j]] * reps, axis=1)
        zr = (yr * tc - yi * ts).astype(BF16)
        zi = (yr * ts + yi * tc).astype(BF16)
        half = mr_ref.shape[0] // 2
        o_re = _dot(mr_ref[:half], zr) + _dot(mi_ref[:half], zi)
        o_im = _dot(mr_ref[half:], zr) + _dot(mi_ref[half:], zi)
        ocat = jnp.concatenate([o_re, o_im], axis=1).astype(BF16)
        res = _dot(ocat, cs_ref[...])
        o_ref[:, j * tile:(j + 1) * tile, :] = res.reshape(half // tile, tile, width)


def _dft_small_kernel(f_ref, cs_ref, x_ref, o_ref):
    x2 = _dot(f_ref[...], x_ref[...].astype(BF16))
    half = x2.shape[0] // 2
    ocat = jnp.concatenate([x2[:half], x2[half:]], axis=1).astype(BF16)
    o_ref[...] = _dot(ocat, cs_ref[...])


def _bf16_const(a):
    return jnp.asarray(np.asarray(a, np.float32), F32).astype(BF16)


def _dft_cos_sin(n):
    idx = np.arange(n)
    ang = 2.0 * np.pi * ((idx[:, None] * idx[None, :]) % n) / n
    return np.cos(ang), np.sin(ang)


def _channel_dft_const(seq, groups):
    c, s = _dft_cos_sin(FOUR_GROUP_DIM)
    eye = np.eye(groups)
    norm = 1.0 / np.sqrt(seq * FOUR_GROUP_DIM)
    return _bf16_const(np.concatenate([np.kron(eye, c), np.kron(eye, s)], axis=0) * norm)


def _fourier_call(f):
    bsz, seq, w = f.shape
    if seq < GRID_W * SUBLANES:
        c, s = _dft_cos_sin(seq)
        fmat = _bf16_const(np.concatenate([c, -s], axis=0))
        cs = _channel_dft_const(seq, FOUR_GROUPS)
        return pl.pallas_call(
            _dft_small_kernel,
            grid=(bsz,),
            in_specs=[_const_spec(fmat.shape), _const_spec(cs.shape),
                      pl.BlockSpec((None, seq, w), lambda b: (b, 0, 0))],
            out_specs=pl.BlockSpec((None, seq, w), lambda b: (b, 0, 0)),
            out_shape=jax.ShapeDtypeStruct((bsz, seq, w), F32),
            compiler_params=_cparams(1),
            name="dft_small",
        )(fmat, cs, f)

    rows = seq // GRID_W
    n_split = 2
    wh = w // n_split
    tile = SUBLANES
    eye = np.eye(tile)
    ca, sa = _dft_cos_sin(rows)
    fa = np.concatenate([ca, -sa], axis=0)
    ka = _bf16_const(np.einsum("ra,ij->iraj", fa, eye).reshape(tile * 2 * rows, rows * tile))
    cb, sb = _dft_cos_sin(GRID_W)
    expand = lambda m: np.einsum("pb,ij->pibj", m, eye).reshape(GRID_W * tile, GRID_W * tile)
    mr = _bf16_const(np.concatenate([expand(cb), expand(-sb)], axis=0))
    mi = _bf16_const(np.concatenate([expand(sb), expand(cb)], axis=0))
    cs = _channel_dft_const(seq, FOUR_GROUPS // n_split)
    ang = 2.0 * np.pi * (np.arange(GRID_W)[:, None] * np.arange(rows)[None, :]) / seq
    tw = lambda t: jnp.asarray(np.repeat(
        t.reshape(GRID_W, rows // tile, tile).transpose(1, 0, 2).reshape(rows // tile, GRID_W * tile)[:, :, None],
        LANES, axis=2), F32)
    tc, ts = tw(np.cos(ang)), tw(-np.sin(ang))
    o = pl.pallas_call(
        _dft_grid_kernel,
        grid=(bsz, n_split),
        in_specs=[_const_spec(ka.shape), _const_spec(mr.shape), _const_spec(mi.shape), _const_spec(cs.shape),
                  _const_spec(tc.shape), _const_spec(ts.shape),
                  pl.BlockSpec((None, rows, GRID_W, wh), lambda b, j: (b, 0, 0, j))],
        out_specs=pl.BlockSpec((None, GRID_W, rows, wh), lambda b, j: (b, 0, 0, j)),
        out_shape=jax.ShapeDtypeStruct((bsz, GRID_W, rows, w), F32),
        scratch_shapes=[pltpu.VMEM((GRID_W, 2 * rows, wh), F32)],
        compiler_params=_cparams(2),
        name="dft_grid",
    )(ka, mr, mi, cs, tc, ts, f.reshape(bsz, rows, GRID_W, w))
    return o.reshape(bsz, seq, w)


def _merge_kernel(mod_ref, x_ref, g1_ref, om_ref, og_ref, of_ref, wg_ref, wbm_ref, wbg_ref, wbf_ref, wo_ref,
                  o_ref, *, mod_row):
    d = x_ref.shape[-1]
    b = pl.program_id(0) if mod_row is None else mod_row
    shift = mod_ref[pl.ds(b, 1), 0:d]
    scale = mod_ref[pl.ds(b, 1), d:2 * d]
    gate1 = mod_ref[pl.ds(b, 1), 2 * d:3 * d]
    x = x_ref[...]
    h = _modulate(x, g1_ref[...], shift, scale).astype(BF16)
    y = None
    branches = ((om_ref[...], wbm_ref), (og_ref[...], wbg_ref), (of_ref[...].astype(BF16), wbf_ref))
    for n, (o, w_ref) in enumerate(branches):
        gate = 1.0 / (1.0 + jnp.exp(-_dot_nt(h, wg_ref[n * d:(n + 1) * d, :])))
        t = gate * _dot(o, w_ref[...])
        y = t if y is None else y + t
    o_ref[...] = x + gate1 * _dot(y.astype(BF16), wo_ref[...])


def _merge_call(x, mod, o_mla, o_gqa, o_four, wp, *, mod_row, tm):
    bsz, seq, d = x.shape
    row = lambda width: pl.BlockSpec((None, tm, width), lambda b, i: (b, i, 0))
    return pl.pallas_call(
        functools.partial(_merge_kernel, mod_row=mod_row),
        grid=(bsz, seq // tm),
        in_specs=[_const_spec(mod.shape), row(d), _const_spec((1, d)),
                  row(o_mla.shape[-1]), row(o_gqa.shape[-1]), row(o_four.shape[-1]),
                  _const_spec(wp["w_gate"].shape), _layer_spec(wp["w_br_mla"], wp["layer"]),
                  _layer_spec(wp["w_br_gqa"], wp["layer"]), _layer_spec(wp["w_four"], wp["layer"]),
                  _layer_spec(wp["w_o"], wp["layer"])],
        out_specs=row(d),
        out_shape=jax.ShapeDtypeStruct(x.shape, F32),
        compiler_params=_cparams(2),
        name="merge",
    )(mod, x, wp["g_norm1"], o_mla, o_gqa, o_four, wp["w_gate"], wp["w_br_mla"], wp["w_br_gqa"],
      wp["w_four"], wp["w_o"])


def _ffn_kernel(mod_ref, xp_ref, x_ref, xn_ref, g2_ref, wup_ref, cw_ref, cb_ref, wdn_ref, o_ref, *, mod_row):
    d = x_ref.shape[-1]
    tm = x_ref.shape[0]
    halo = xp_ref.shape[0]
    b = pl.program_id(0) if mod_row is None else mod_row
    i = pl.program_id(1)
    shift = mod_ref[pl.ds(b, 1), 3 * d:4 * d]
    scale = mod_ref[pl.ds(b, 1), 4 * d:5 * d]
    gate2 = mod_ref[pl.ds(b, 1), 5 * d:6 * d]
    x = x_ref[...]
    xe = jnp.concatenate([xp_ref[...], x, xn_ref[...]], axis=0)
    he = _modulate(xe, g2_ref[...], shift, scale)
    rid = lax.broadcasted_iota(jnp.int32, (tm + 2 * halo, 1), 0)
    outside = ((rid < halo) & (i == 0)) | ((rid >= tm + halo) & (i == pl.num_programs(1) - 1))
    he = jnp.where(outside, 0.0, he).astype(BF16)
    u = _dot(he, wup_ref[...])
    n = tm + 2 * halo
    up = pltpu.roll(u, 1, axis=0)[halo:halo + tm]
    dn = pltpu.roll(u, n - 1, axis=0)[halo:halo + tm]
    u = cw_ref[0:1, :] * up + cw_ref[1:2, :] * u[halo:halo + tm] + cw_ref[2:3, :] * dn + cb_ref[...]
    ff = u.shape[-1] // 2
    a, v = u[:, :ff], u[:, ff:]
    act = (a * (1.0 / (1.0 + jnp.exp(-a))) * v).astype(BF16)
    o_ref[...] = x + gate2 * _dot(act, wdn_ref[...])


def _ffn_call(x, mod, wp, *, mod_row, tm):
    bsz, seq, d = x.shape
    halo = SUBLANES
    nblk = seq // halo
    per = tm // halo
    return pl.pallas_call(
        functools.partial(_ffn_kernel, mod_row=mod_row),
        grid=(bsz, seq // tm),
        in_specs=[_const_spec(mod.shape),
                  pl.BlockSpec((None, halo, d), lambda b, i: (b, jnp.maximum(i * per - 1, 0), 0)),
                  pl.BlockSpec((None, tm, d), lambda b, i: (b, i, 0)),
                  pl.BlockSpec((None, halo, d), lambda b, i: (b, jnp.minimum((i + 1) * per, nblk - 1), 0)),
                  _const_spec((1, d)), _layer_spec(wp["w_up"], wp["layer"]), _const_spec(wp["conv_w"].shape),
                  _const_spec(wp["conv_b"].shape), _layer_spec(wp["w_down"], wp["layer"])],
        out_specs=pl.BlockSpec((None, tm, d), lambda b, i: (b, i, 0)),
        out_shape=jax.ShapeDtypeStruct(x.shape, F32),
        compiler_params=_cparams(2),
        name="ffn",
    )(mod, x, x, x, wp["g_norm2"], wp["w_up"], wp["conv_w"], wp["conv_b"], wp["w_down"])


def _slots(w, n_slots, width, offset=0):
    k = w.shape[0]
    w3 = w.reshape(k, n_slots, width)
    w3 = jnp.pad(w3, ((0, 0), (0, 0), (offset, LANES - offset - width)))
    return w3.reshape(k, n_slots * LANES)


def _slots_t(wt, n_slots, width, offset=0):
    k = wt.shape[1]
    w3 = wt.reshape(n_slots, width, k)
    w3 = jnp.pad(w3, ((0, 0), (offset, LANES - offset - width), (0, 0)))
    return w3.reshape(n_slots * LANES, k)


def _prep_layer(l, p):
    d = p["w_in"].shape[1]
    w_t = jnp.swapaxes(p["w_in"], 1, 2)[l].astype(BF16)
    o = 0
    c_kv = w_t[o:o + MLA_KV_RANK]; o += MLA_KV_RANK
    k_rope = w_t[o:o + MLA_ROPE]; o += MLA_ROPE
    k_g = w_t[o:o + GQA_KV_HEADS * GQA_DIM]; o += GQA_KV_HEADS * GQA_DIM
    v_g = w_t[o:o + GQA_KV_HEADS * GQA_DIM]; o += GQA_KV_HEADS * GQA_DIM
    c_q = w_t[o:o + MLA_Q_RANK]; o += MLA_Q_RANK
    q_g = w_t[o:o + GQA_HEADS * GQA_DIM]; o += GQA_HEADS * GQA_DIM
    four = w_t[o:o + FOUR_WIDTH]; o += FOUR_WIDTH
    gates = w_t[o:o + N_BRANCH * d]
    dup = lambda w, n: _slots_t(w, n, GQA_DIM, 0) + _slots_t(w, n, GQA_DIM, GQA_DIM)
    w1 = jnp.concatenate([
        c_kv,
        _slots_t(k_rope, 1, MLA_ROPE, MLA_NOPE),
        dup(k_g, GQA_KV_HEADS),
        _slots_t(v_g, GQA_KV_HEADS, GQA_DIM),
        c_q,
        dup(q_g, GQA_HEADS),
        four], axis=0)

    w_ukv = p["w_ukv"][l].reshape(MLA_KV_RANK, MLA_HEADS, MLA_NOPE + MLA_V)
    k_nope = w_ukv[:, :, :MLA_NOPE].reshape(MLA_KV_RANK, MLA_HEADS * MLA_NOPE)
    v_mla = w_ukv[:, :, MLA_NOPE:].reshape(MLA_KV_RANK, MLA_HEADS * MLA_V)
    w_ukv2 = jnp.concatenate([_slots(k_nope, MLA_HEADS, MLA_NOPE), _slots(v_mla, MLA_HEADS, MLA_V)],
                             axis=1).astype(BF16)
    w_uq2 = _slots(p["w_uq"][l], MLA_HEADS, MLA_QK).astype(BF16)

    pad = lambda g: jnp.pad(g, (0, LANES - g.shape[0]))
    onehot = lambda lane: jnp.zeros((LANES,), F32).at[lane].set(1.0)
    bound_m = (MLA_QK ** 0.5) * jnp.max(jnp.abs(p["g_qn_mla"][l])) * jnp.max(jnp.abs(p["g_kn_mla"][l])) * (1 + 2.0 ** -6)
    bound_g = (GQA_DIM ** 0.5) * jnp.max(jnp.abs(p["g_qn_gqa"][l])) * jnp.max(jnp.abs(p["g_kn_gqa"][l])) * (1 + 2.0 ** -6)
    safe_m, safe_g = bound_m <= SAFE_SCORE_BOUND, bound_g <= SAFE_SCORE_BOUND
    swap = lambda g, lo, n: jnp.concatenate([g[:lo], g[lo + n:lo + 2 * n], g[lo:lo + n], g[lo + 2 * n:]])
    c_m, c_g = MLA_QK ** -0.5 * LOG2E, GQA_DIM ** -0.5 * LOG2E
    rope_only = (jnp.arange(LANES) >= MLA_NOPE).astype(F32)
    rows = [None] * 10
    rows[G_KN_MLA] = pad(p["g_kn_mla"][l])
    rows[G_QN_MLA] = pad(p["g_qn_mla"][l]) * c_m
    rows[G_QN_MLA_SW] = swap(pad(p["g_qn_mla"][l]), MLA_NOPE, MLA_ROPE // 2) * rope_only * c_m
    rows[G_KN_GQA] = pad(p["g_kn_gqa"][l])
    rows[G_KN_GQA_SW] = swap(pad(p["g_kn_gqa"][l]), 0, GQA_DIM // 2)
    rows[G_QN_GQA] = pad(p["g_qn_gqa"][l]) * c_g
    rows[G_QN_GQA_SW] = swap(pad(p["g_qn_gqa"][l]), 0, GQA_DIM // 2) * c_g
    rows[AUG_K_MLA] = onehot(MLA_QK) * jnp.where(safe_m, -bound_m * LOG2E, 0.0)
    rows[AUG_K_GQA] = onehot(GQA_DIM) * jnp.where(safe_g, -bound_g * LOG2E, 0.0)
    rows[AUG_V] = onehot(SUM_LANE)
    gslot = jnp.stack(rows)
    gslot = jnp.pad(gslot, ((0, GSLOT_ROWS - gslot.shape[0]), (0, 0)))
    return {
        "w1": w1, "w_ukv": w_ukv2, "w_uq": w_uq2, "gslot": gslot, "safe_mla": safe_m, "safe_gqa": safe_g,
        "g_norm1": p["g_norm1"][l][None], "g_norm2": p["g_norm2"][l][None],
        "g_ckv": p["g_ckv"][l][None], "g_cq": p["g_cq"][l][None],
        "w_gate": gates, "conv_w": p["conv_w"][l], "conv_b": p["conv_b"][l][None],
        "layer": l, **{name: p["bf16"][name] for name in STACKED},
    }


def _rope_angles(seq, rot_dim):
    rows = seq // GRID_W
    row = np.repeat(np.arange(rows, dtype=np.float32), GRID_W)
    col = np.tile(np.arange(GRID_W, dtype=np.float32), rows)
    n_f = rot_dim // 4
    inv = (ROPE_BASE ** (-np.arange(n_f, dtype=np.float32) / n_f)).astype(np.float32)
    ang = jnp.asarray(np.concatenate([row[:, None] * inv, col[:, None] * inv], axis=-1), F32)
    return jnp.cos(ang), jnp.sin(ang)


def _rope_tables_mla(seq):
    cos, sin = _rope_angles(seq, MLA_ROPE)
    half = MLA_ROPE // 2
    z = lambda n: jnp.zeros((seq, n), F32)
    o = lambda n: jnp.ones((seq, n), F32)
    rest = LANES - MLA_QK
    return jnp.stack([
        jnp.concatenate([o(MLA_NOPE), cos, cos, o(rest)], axis=1),
        jnp.concatenate([z(MLA_NOPE), -sin, z(half), z(rest)], axis=1),
        jnp.concatenate([z(MLA_NOPE), z(half), sin, z(rest)], axis=1)])


def _rope_tables_gqa(seq):
    cos, sin = _rope_angles(seq, GQA_DIM)
    z = jnp.zeros((seq, LANES - GQA_DIM), F32)
    return jnp.stack([jnp.concatenate([cos, cos, z], axis=1), jnp.concatenate([-sin, sin, z], axis=1)])


def kernel(x, c, ctx, c_ctx, w_mod, b_mod, g_norm1, g_norm2, w_in, g_cq, g_ckv, w_uq, w_ukv, g_qn_mla, g_kn_mla,
           g_qn_gqa, g_kn_gqa, w_br_mla, w_br_gqa, w_four, w_o, w_up, conv_w, conv_b, w_down):
    params = dict(w_in=w_in, g_norm1=g_norm1, g_norm2=g_norm2, g_cq=g_cq, g_ckv=g_ckv, w_uq=w_uq, w_ukv=w_ukv,
                  g_qn_mla=g_qn_mla, g_kn_mla=g_kn_mla, g_qn_gqa=g_qn_gqa, g_kn_gqa=g_kn_gqa,
                  w_br_mla=w_br_mla, w_br_gqa=w_br_gqa, w_four=w_four, w_o=w_o, w_up=w_up, conv_w=conv_w,
                  conv_b=conv_b, w_down=w_down)
    params["bf16"] = {name: params[name].astype(BF16) for name in STACKED}
    bsz, seq, d = x.shape
    ctx_len = ctx.shape[1]
    depth = w_mod.shape[0]
    tm = 512
    tm_c = min(tm, ctx_len)
    tq = min(1024, seq)

    c_all = jnp.concatenate([c, c_ctx[None], jnp.zeros((SUBLANES - (bsz + 1) % SUBLANES, d), F32)], axis=0)
    mod_all = _mod_call(c_all, w_mod, b_mod)
    tabs = (_rope_tables_mla(seq), _rope_tables_gqa(seq))
    tabs_c = tuple(t[:, :ctx_len] for t in tabs)

    xc = ctx
    for l in range(depth):
        wp = _prep_layer(l, params)
        mod = mod_all[l]
        update_ctx = l < depth - 1
        km, vm, kg, vg, qm, qg, four = _inproj_call(x, mod, wp, tabs, mod_row=None, rope=True, kv_only=False, tm=tm)
        c_out = _inproj_call(xc, mod, wp, tabs_c, mod_row=bsz, rope=False, kv_only=not update_ctx, tm=tm_c)
        kmc, vmc, kgc, vgc = c_out[:4]
        qtab_m = _query_tables(wp["gslot"], tabs[0], G_QN_MLA, G_QN_MLA_SW, seq)
        qtab_g = _query_tables(wp["gslot"], tabs[1], G_QN_GQA, G_QN_GQA_SW, seq)
        o_mla = _attn_call(qm, [(kmc, vmc), (km, vm)], wp["safe_mla"], qtab_m, shared_kv=False, tq=tq, name="attn_mla")
        o_gqa = _attn_call(qg, [(kgc, vgc), (kg, vg)], wp["safe_gqa"], qtab_g, shared_kv=True, tq=tq, name="attn_gqa")
        o_four = _fourier_call(four)
        x = _merge_call(x, mod, o_mla, o_gqa, o_four, wp, mod_row=None, tm=tm)
        x = _ffn_call(x, mod, wp, mod_row=None, tm=tm)
        if update_ctx:
            qmc, qgc, fourc = c_out[4:]
            qtab_mc = _query_tables(wp["gslot"], None, G_QN_MLA, G_QN_MLA_SW, ctx_len)
            qtab_gc = _query_tables(wp["gslot"], None, G_QN_GQA, G_QN_GQA_SW, ctx_len)
            oc_mla = _attn_call(qmc, [(kmc, vmc)], wp["safe_mla"], qtab_mc, shared_kv=False, tq=tm_c, name="attn_mla_ctx")
            oc_gqa = _attn_call(qgc, [(kgc, vgc)], wp["safe_gqa"], qtab_gc, shared_kv=True, tq=tm_c, name="attn_gqa_ctx")
            oc_four = _fourier_call(fourc)
            xc = _merge_call(xc, mod, oc_mla, oc_gqa, oc_four, wp, mod_row=bsz, tm=tm_c)
            xc = _ffn_call(xc, mod, wp, mod_row=bsz, tm=tm_c)
    return x
```
